```python
import math
import jax, jax.numpy as jnp
from jax import lax
import numpy as np

D_MODEL = 2048
BATCH = 8
SEQ = 2048
DEPTH = 4

GRID_W = 64
CTX_LEN = 256
N_MIXERS = 4
CHUNK = 64
CONV_W = 3
D_FF = 4 * D_MODEL
NORM_EPS = 1e-6

SSD_D_INNER = 2 * D_MODEL
SSD_HEAD_DIM = 64
SSD_HEADS = SSD_D_INNER // SSD_HEAD_DIM
SSD_GROUPS = 8
SSD_STATE = 128
SSD_CONV_CH = SSD_D_INNER + 2 * SSD_GROUPS * SSD_STATE
SSD_IN = SSD_D_INNER + SSD_CONV_CH + 2 * SSD_HEADS

RET_HEADS = 8
RET_QK_DIM = D_MODEL // RET_HEADS
RET_V_DIM = 2 * RET_QK_DIM
RET_DV = RET_HEADS * RET_V_DIM
RET_IN = 2 * D_MODEL + 2 * RET_DV
ROPE_BASE = 10000.0

HGRN_EXPAND = 128
HGRN_HEADS = D_MODEL // HGRN_EXPAND
HGRN_IN = 5 * D_MODEL

GDN_HEAD_DIM = 128
GDN_K_HEADS = D_MODEL // GDN_HEAD_DIM
GDN_V_HEADS = 2 * GDN_K_HEADS
GDN_DK = GDN_K_HEADS * GDN_HEAD_DIM
GDN_DV = GDN_V_HEADS * GDN_HEAD_DIM
GDN_CONV_CH = 2 * GDN_DK + GDN_DV
GDN_IN = GDN_CONV_CH + GDN_DV + 4 * GDN_V_HEADS

kernel_name = 'hybrid_bidir_recurrent_dit_block'


def _rmsnorm(x, g, eps=NORM_EPS):
    x32 = x.astype(jnp.float32)
    y = x32 * lax.rsqrt(jnp.mean(x32 * x32, axis=-1, keepdims=True) + eps)
    return y.astype(x.dtype) * g


def _adaln(x, g, shift, scale):
    return _rmsnorm(x, g) * (1 + scale) + shift


def _l2norm(x, eps=1e-6):
    x32 = x.astype(jnp.float32)
    return x32 * lax.rsqrt(jnp.sum(x32 * x32, axis=-1, keepdims=True) + eps)


def _sq_relu_mlp(h, w1, w2):
    return jnp.square(jax.nn.relu(h @ w1)) @ w2


def _dwconv(u, w):
    return lax.conv_general_dilated(u, w[:, None, :], window_strides=(1,),
                                    padding=[(CONV_W // 2, CONV_W // 2)],
                                    dimension_numbers=('NWC', 'WIO', 'NWC'),
                                    feature_group_count=u.shape[-1])


def _conv_split(u, w, lc):
    return jnp.concatenate([_dwconv(u[:, :lc], w), _dwconv(u[:, lc:], w)], axis=1)


def _rev(t, lc):
    return jnp.concatenate([jnp.flip(t[:, :lc], 1), jnp.flip(t[:, lc:], 1)], axis=1)


def _split_out(out, lc, keep_ctx):
    if keep_ctx:
        return out[:, :lc], out[:, lc:]
    return None, out


def _to_chunks(t):
    b, n = t.shape[:2]
    return jnp.moveaxis(t.reshape((b, n // CHUNK, CHUNK) + t.shape[2:]), 1, 0)


def _from_chunks(t):
    nc, b, q = t.shape[:3]
    return jnp.moveaxis(t, 0, 1).reshape((b, nc * q) + t.shape[3:])


def _chunk_masks():
    idx = jnp.arange(CHUNK)
    return idx[:, None] >= idx[None, :], idx[:, None] > idx[None, :]


def _scalar_decay_scan(q, k, v, log_a):
    f32 = jnp.float32
    q, k, v, log_a = (t.astype(f32) for t in (q, k, v, log_a))
    bsz, _, g, n = q.shape
    r, p = v.shape[-2:]
    incl, _ = _chunk_masks()

    def body(s, xs):
        qc, kc, vc, la = xs
        cum = jnp.cumsum(la, axis=1)
        cum_t = jnp.moveaxis(cum, 1, -1)
        seg = cum_t[..., :, None] - cum_t[..., None, :]
        scores = jnp.einsum('btgn,bsgn->bgts', qc, kc)
        attn = scores[:, :, None] * jnp.exp(jnp.where(incl, seg, -jnp.inf))
        y = jnp.einsum('bgrts,bsgrp->btgrp', attn, vc)
        y = y + jnp.einsum('btgn,bgrnp->btgrp', qc, s) * jnp.exp(cum)[..., None]
        to_end = jnp.exp(cum[:, -1:] - cum)
        s = jnp.exp(cum[:, -1])[..., None, None] * s + jnp.einsum('bsgn,bsgr,bsgrp->bgrnp', kc, to_end, vc)
        return s, y

    s0 = jnp.zeros((bsz, g, r, n, p), f32)
    _, y = lax.scan(body, s0, tuple(_to_chunks(t) for t in (q, k, v, log_a)))
    return _from_chunks(y)


def _vector_decay_scan(q, k, v, log_f):
    f32 = jnp.float32
    q, k, v, log_f = (t.astype(f32) for t in (q, k, v, log_f))
    bsz, _, h, kd = q.shape
    vd = v.shape[-1]
    incl, _ = _chunk_masks()

    def body(s, xs):
        qc, kc, vc, lf = xs
        cum = jnp.cumsum(lf, axis=1)
        seg = cum[:, :, None] - cum[:, None, :]
        decay = jnp.exp(jnp.where(incl[:, :, None, None], seg, -jnp.inf))
        attn = jnp.einsum('bthk,bshk,btshk->bhts', qc, kc, decay)
        y = jnp.einsum('bhts,bshv->bthv', attn, vc)
        y = y + jnp.einsum('bthk,bhkv->bthv', qc * jnp.exp(cum), s)
        s = jnp.exp(cum[:, -1])[..., None] * s + jnp.einsum('bshk,bshv->bhkv', kc * jnp.exp(cum[:, -1:] - cum), vc)
        return s, y

    s0 = jnp.zeros((bsz, h, kd, vd), f32)
    _, y = lax.scan(body, s0, tuple(_to_chunks(t) for t in (q, k, v, log_f)))
    return _from_chunks(y)


def _delta_scan(q, k, v, beta, log_a):
    f32 = jnp.float32
    q, k, v, beta, log_a = (t.astype(f32) for t in (q, k, v, beta, log_a))
    bsz, _, g, kd = q.shape
    r, vd = v.shape[-2:]
    incl, strict = _chunk_masks()

    def body(s, xs):
        qc, kc, vc, bc, la = xs
        cum = jnp.cumsum(la, axis=1)
        cum_t = jnp.moveaxis(cum, 1, -1)
        seg = cum_t[..., :, None] - cum_t[..., None, :]
        beta_t = jnp.moveaxis(bc, 1, -1)
        kk = jnp.einsum('btgk,bsgk->bgts', kc, kc)
        lower = beta_t[..., :, None] * kk[:, :, None] * jnp.exp(jnp.where(strict, seg, -jnp.inf))
        rhs_v = jnp.moveaxis(vc * bc[..., None], 1, 3)
        rhs_k = jnp.moveaxis(kc[:, :, :, None, :] * (bc * jnp.exp(cum))[..., None], 1, 3)
        sol = lax.linalg.triangular_solve(lower, jnp.concatenate([rhs_v, rhs_k], axis=-1),
                                          left_side=True, lower=True, unit_diagonal=True)
        u, w = sol[..., :vd], sol[..., vd:]
        v_new = u - jnp.einsum('bgrtk,bgrkv->bgrtv', w, s)
        qk = jnp.einsum('btgk,bsgk->bgts', qc, kc)
        attn = qk[:, :, None] * jnp.exp(jnp.where(incl, seg, -jnp.inf))
        y = jnp.einsum('bgrts,bgrsv->btgrv', attn, v_new)
        y = y + jnp.einsum('btgk,bgrkv->btgrv', qc, s) * jnp.exp(cum)[..., None]
        to_end = jnp.exp(cum_t[..., -1:] - cum_t)
        s = jnp.exp(cum_t[..., -1])[..., None, None] * s + jnp.einsum('bsgk,bgrs,bgrsv->bgrkv', kc, to_end, v_new)
        return s, y

    s0 = jnp.zeros((bsz, g, r, kd, vd), f32)
    _, y = lax.scan(body, s0, tuple(_to_chunks(t) for t in (q, k, v, beta, log_a)))
    return _from_chunks(y)


def _rope_2d(t, rows):
    f32 = jnp.float32
    pos = jnp.arange(rows * GRID_W)
    row = (pos // GRID_W).astype(f32)
    col = (pos % GRID_W).astype(f32)
    half = t.shape[-1] // 2
    inv_freq = ROPE_BASE ** (-jnp.arange(0, half, 2, dtype=f32) / half)

    def rot(u, p):
        ang = p[:, None] * inv_freq
        cos, sin = jnp.cos(ang)[:, None, :], jnp.sin(ang)[:, None, :]
        u1, u2 = jnp.split(u, 2, axis=-1)
        return jnp.concatenate([u1 * cos - u2 * sin, u2 * cos + u1 * sin], axis=-1)

    t32 = t.astype(f32)
    return jnp.concatenate([rot(t32[..., :half], row), rot(t32[..., half:], col)], axis=-1).astype(t.dtype)


def _ssd_mixer(h_ctx, h_lat, w_in, conv_w, conv_b, dt_bias, a_log, d_skip, norm_g, w_out, keep_ctx):
    f32 = jnp.float32
    lc = h_ctx.shape[1]
    u = jnp.concatenate([h_ctx, h_lat], axis=1) @ w_in
    bsz, t = u.shape[:2]
    z, xbc, dt = jnp.split(u, [SSD_D_INNER, SSD_D_INNER + SSD_CONV_CH], axis=-1)
    xbc = jax.nn.silu(_conv_split(xbc, conv_w, lc) + conv_b)
    xs, bm, cm = jnp.split(xbc, [SSD_D_INNER, SSD_D_INNER + SSD_GROUPS * SSD_STATE], axis=-1)
    r = SSD_HEADS // SSD_GROUPS
    xs = xs.reshape(bsz, t, SSD_GROUPS, r, SSD_HEAD_DIM)
    bm = bm.reshape(bsz, t, SSD_GROUPS, SSD_STATE)
    cm = cm.reshape(bsz, t, SSD_GROUPS, SSD_STATE)
    dt = jax.nn.softplus(dt.reshape(bsz, t, 2, SSD_HEADS).astype(f32) + dt_bias.astype(f32))
    log_a = -jnp.exp(a_log.astype(f32)) * dt
    grp = lambda a: a.reshape(bsz, t, SSD_GROUPS, r)
    y = _scalar_decay_scan(cm, bm, xs * grp(dt[:, :, 0])[..., None], grp(log_a[:, :, 0]))
    y = y + _rev(_scalar_decay_scan(_rev(cm, lc), _rev(bm, lc), _rev(xs * grp(dt[:, :, 1])[..., None], lc),
                                    _rev(grp(log_a[:, :, 1]), lc)), lc)
    y = y + d_skip.reshape(SSD_GROUPS, r)[..., None] * xs
    start = 0 if keep_ctx else lc
    n = t - start
    y = y.reshape(bsz, t, SSD_D_INNER)[:, start:].astype(h_lat.dtype) * jax.nn.silu(z[:, start:])
    y = _rmsnorm(y.reshape(bsz, n, SSD_GROUPS, -1), norm_g.reshape(SSD_GROUPS, -1)).reshape(bsz, n, SSD_D_INNER)
    return _split_out(y @ w_out, lc, keep_ctx)


def _retention_mixer(h_ctx, h_lat, w_in, log_decay, w_out, rows, keep_ctx):
    f32 = jnp.float32
    lc = h_ctx.shape[1]
    u = jnp.concatenate([h_ctx, h_lat], axis=1) @ w_in
    bsz, t = u.shape[:2]
    q, k, v, g = jnp.split(u, [D_MODEL, 2 * D_MODEL, 2 * D_MODEL + RET_DV], axis=-1)
    q = q.reshape(bsz, t, RET_HEADS, RET_QK_DIM)
    k = k.reshape(bsz, t, RET_HEADS, RET_QK_DIM) * RET_QK_DIM ** -0.5
    q = jnp.concatenate([q[:, :lc], _rope_2d(q[:, lc:], rows)], axis=1)
    k = jnp.concatenate([k[:, :lc], _rope_2d(k[:, lc:], rows)], axis=1)
    v = v.reshape(bsz, t, RET_HEADS, 1, RET_V_DIM)
    ld_f = jnp.broadcast_to(log_decay[0].astype(f32)[:, None], (bsz, t, RET_HEADS, 1))
    ld_b = jnp.broadcast_to(log_decay[1].astype(f32)[:, None], (bsz, t, RET_HEADS, 1))
    y = _scalar_decay_scan(q, k, v, ld_f)
    y = y + _rev(_scalar_decay_scan(_rev(q, lc), _rev(k, lc), _rev(v, lc), ld_b), lc)
    start = 0 if keep_ctx else lc
    n = t - start
    y = y[:, start:].reshape(bsz, n, RET_HEADS, RET_V_DIM)
    mu = jnp.mean(y, axis=-1, keepdims=True)
    var = jnp.mean(jnp.square(y - mu), axis=-1, keepdims=True)
    y = ((y - mu) * lax.rsqrt(var + NORM_EPS)).reshape(bsz, n, RET_DV).astype(h_lat.dtype)
    y = y * jax.nn.silu(g[:, start:])
    return _split_out(y @ w_out, lc, keep_ctx)


def _lower_bound(lb_logits, layer):
    p = jax.nn.softmax(lb_logits.astype(jnp.float32), axis=0)
    return jnp.cumsum(p, axis=0)[layer] - p[0]


def _hgrn2_mixer(h_ctx, h_lat, w_in, lb, norm_g, w_out, keep_ctx):
    f32 = jnp.float32
    lc = h_ctx.shape[1]
    u = jnp.concatenate([h_ctx, h_lat], axis=1) @ w_in
    bsz, t = u.shape[:2]
    q, f_f, f_b, i, g = jnp.split(u, 5, axis=-1)
    shp = (bsz, t, HGRN_HEADS, HGRN_EXPAND)
    q, i = q.reshape(shp), i.reshape(shp)
    lb = lb.reshape(HGRN_HEADS, HGRN_EXPAND)

    def gates(f):
        f = f.reshape(shp).astype(f32)
        log_f = jnp.logaddexp(jnp.log(lb), jnp.log1p(-lb) + jax.nn.log_sigmoid(f))
        return log_f, (1 - lb) * jax.nn.sigmoid(-f)

    lf_f, k_f = gates(f_f)
    lf_b, k_b = gates(f_b)
    y = _vector_decay_scan(q, k_f, i, lf_f)
    y = y + _rev(_vector_decay_scan(_rev(q, lc), _rev(k_b, lc), _rev(i, lc), _rev(lf_b, lc)), lc)
    start = 0 if keep_ctx else lc
    n = t - start
    y = _rmsnorm(y[:, start:].astype(h_lat.dtype), norm_g.reshape(HGRN_HEADS, HGRN_EXPAND))
    y = (y * jax.nn.silu(g[:, start:].reshape(bsz, n, HGRN_HEADS, HGRN_EXPAND))).reshape(bsz, n, D_MODEL)
    return _split_out(y @ w_out, lc, keep_ctx)


def _gdn_mixer(h_ctx, h_lat, w_in, conv_w, dt_bias, a_log, norm_g, w_out, keep_ctx):
    f32 = jnp.float32
    lc = h_ctx.shape[1]
    u = jnp.concatenate([h_ctx, h_lat], axis=1) @ w_in
    bsz, t = u.shape[:2]
    qkv, z, bt, a = jnp.split(u, [GDN_CONV_CH, GDN_CONV_CH + GDN_DV, GDN_CONV_CH + GDN_DV + 2 * GDN_V_HEADS], axis=-1)
    qkv = jax.nn.silu(_conv_split(qkv, conv_w, lc))
    q, k, v = jnp.split(qkv, [GDN_DK, 2 * GDN_DK], axis=-1)
    r = GDN_V_HEADS // GDN_K_HEADS
    q = _l2norm(q.reshape(bsz, t, GDN_K_HEADS, GDN_HEAD_DIM)) * GDN_HEAD_DIM ** -0.5
    k = _l2norm(k.reshape(bsz, t, GDN_K_HEADS, GDN_HEAD_DIM))
    v = v.reshape(bsz, t, GDN_K_HEADS, r, GDN_HEAD_DIM)
    beta = jax.nn.sigmoid(bt.reshape(bsz, t, 2, GDN_K_HEADS, r).astype(f32))
    log_a = -jnp.exp(a_log.astype(f32)).reshape(2, GDN_K_HEADS, r) * jax.nn.softplus(
        a.reshape(bsz, t, 2, GDN_K_HEADS, r).astype(f32) + dt_bias.astype(f32).reshape(2, GDN_K_HEADS, r))
    y = _delta_scan(q, k, v, beta[:, :, 0], log_a[:, :, 0])
    y = y + _rev(_delta_scan(_rev(q, lc), _rev(k, lc), _rev(v, lc), _rev(beta[:, :, 1], lc),
                             _rev(log_a[:, :, 1], lc)), lc)
    start = 0 if keep_ctx else lc
    n = t - start
    y = y[:, start:].reshape(bsz, n, GDN_V_HEADS, GDN_HEAD_DIM).astype(h_lat.dtype)
    y = _rmsnorm(y, norm_g) * jax.nn.silu(z[:, start:].reshape(bsz, n, GDN_V_HEADS, GDN_HEAD_DIM))
    return _split_out(y.reshape(bsz, n, GDN_DV) @ w_out, lc, keep_ctx)


def _n_occ(m):
    return len(range(m, DEPTH, N_MIXERS))


def _fwd_setup_inputs(seed: int = 0) -> dict:
    key = jax.random.key(seed)
    ks = iter(jax.random.split(key, 48))
    f32 = jnp.float32
    d = D_MODEL

    def nrm(shape, std):
        return std * jax.random.normal(next(ks), shape, f32)

    def gain(shape):
        return 1.0 + nrm(shape, 0.02)

    def dt_bias(shape):
        dt = jnp.exp(jax.random.uniform(next(ks), shape, f32, math.log(1e-3), math.log(1e-1)))
        return dt + jnp.log(-jnp.expm1(-dt))

    def a_log(shape):
        return jnp.log(jax.random.uniform(next(ks), shape, f32, 1.0, 16.0))

    n_a, n_b, n_c, n_d = (_n_occ(m) for m in range(N_MIXERS))
    ret_base = jnp.log1p(-(2.0 ** (-5.0 - jnp.arange(RET_HEADS, dtype=f32))))
    return {
        'x': nrm((BATCH, SEQ, d), 1.0),
        'c': nrm((BATCH, d), 1.0),
        'ctx': nrm((BATCH, CTX_LEN, d), 1.0),
        'c_ctx': nrm((d,), 1.0),
        'ada_w': nrm((DEPTH, d, 6 * d), 0.5 * d ** -0.5),
        'ada_b': nrm((DEPTH, 6 * d), 0.02),
        'norm_g': gain((DEPTH, 2, d)),
        'mlp_w1': nrm((DEPTH, d, D_FF), d ** -0.5),
        'mlp_w2': nrm((DEPTH, D_FF, d), D_FF ** -0.5),
        'final_g': gain((d,)),
        'ssd_w_in': nrm((n_a, d, SSD_IN), d ** -0.5),
        'ssd_conv_w': nrm((n_a, CONV_W, SSD_CONV_CH), CONV_W ** -0.5),
        'ssd_conv_b': nrm((n_a, SSD_CONV_CH), 0.02),
        'ssd_dt_bias': dt_bias((n_a, 2, SSD_HEADS)),
        'ssd_a_log': a_log((n_a, 2, SSD_HEADS)),
        'ssd_d': gain((n_a, SSD_HEADS)),
        'ssd_norm_g': gain((n_a, SSD_D_INNER)),
        'ssd_w_out': nrm((n_a, SSD_D_INNER, d), SSD_D_INNER ** -0.5),
        'ret_w_in': nrm((n_b, d, RET_IN), d ** -0.5),
        'ret_log_decay': ret_base * jnp.exp(nrm((n_b, 2, RET_HEADS), 0.1)),
        'ret_w_out': nrm((n_b, RET_DV, d), RET_DV ** -0.5),
        'hgrn_w_in': nrm((n_c, d, HGRN_IN), d ** -0.5),
        'hgrn_lb_logits': nrm((DEPTH, d), 0.1),
        'hgrn_norm_g': gain((n_c, d)),
        'hgrn_w_out': nrm((n_c, d, d), d ** -0.5),
        'gdn_w_in': nrm((n_d, d, GDN_IN), d ** -0.5),
        'gdn_conv_w': nrm((n_d, CONV_W, GDN_CONV_CH), CONV_W ** -0.5),
        'gdn_dt_bias': dt_bias((n_d, 2, GDN_V_HEADS)),
        'gdn_a_log': a_log((n_d, 2, GDN_V_HEADS)),
        'gdn_norm_g': gain((n_d, GDN_HEAD_DIM)),
        'gdn_w_out': nrm((n_d, GDN_DV, d), GDN_DV ** -0.5),
    }


def _fwd_reference(x, c, ctx, c_ctx, ada_w, ada_b, norm_g, mlp_w1, mlp_w2, final_g,
              ssd_w_in, ssd_conv_w, ssd_conv_b, ssd_dt_bias, ssd_a_log, ssd_d, ssd_norm_g, ssd_w_out,
              ret_w_in, ret_log_decay, ret_w_out,
              hgrn_w_in, hgrn_lb_logits, hgrn_norm_g, hgrn_w_out,
              gdn_w_in, gdn_conv_w, gdn_dt_bias, gdn_a_log, gdn_norm_g, gdn_w_out):
    bsz = x.shape[0]
    rows = x.shape[1] // GRID_W
    x_lat, x_ctx = x, ctx
    cond = jax.nn.silu(jnp.concatenate([c, c_ctx[None]], axis=0))
    for i in range(DEPTH):
        mixer, occ = i % N_MIXERS, i // N_MIXERS
        keep_ctx = i < DEPTH - 1
        mod = cond @ ada_w[i] + ada_b[i]
        sh1, sc1, g1, sh2, sc2, g2 = jnp.split(mod, 6, axis=-1)
        h_lat = _adaln(x_lat, norm_g[i, 0], sh1[:bsz, None], sc1[:bsz, None])
        h_ctx = _adaln(x_ctx, norm_g[i, 0], sh1[bsz], sc1[bsz])
        if mixer == 0:
            y_ctx, y_lat = _ssd_mixer(h_ctx, h_lat, ssd_w_in[occ], ssd_conv_w[occ], ssd_conv_b[occ], ssd_dt_bias[occ],
                                      ssd_a_log[occ], ssd_d[occ], ssd_norm_g[occ], ssd_w_out[occ], keep_ctx)
        elif mixer == 1:
            y_ctx, y_lat = _retention_mixer(h_ctx, h_lat, ret_w_in[occ], ret_log_decay[occ], ret_w_out[occ],
                                            rows, keep_ctx)
        elif mixer == 2:
            y_ctx, y_lat = _hgrn2_mixer(h_ctx, h_lat, hgrn_w_in[occ], _lower_bound(hgrn_lb_logits, i),
                                        hgrn_norm_g[occ], hgrn_w_out[occ], keep_ctx)
        else:
            y_ctx, y_lat = _gdn_mixer(h_ctx, h_lat, gdn_w_in[occ], gdn_conv_w[occ], gdn_dt_bias[occ],
                                      gdn_a_log[occ], gdn_norm_g[occ], gdn_w_out[occ], keep_ctx)
        x_lat = x_lat + g1[:bsz, None] * y_lat
        h_lat = _adaln(x_lat, norm_g[i, 1], sh2[:bsz, None], sc2[:bsz, None])
        x_lat = x_lat + g2[:bsz, None] * _sq_relu_mlp(h_lat, mlp_w1[i], mlp_w2[i])
        if keep_ctx:
            x_ctx = x_ctx + g1[bsz] * y_ctx
            h_ctx = _adaln(x_ctx, norm_g[i, 1], sh2[bsz], sc2[bsz])
            x_ctx = x_ctx + g2[bsz] * _sq_relu_mlp(h_ctx, mlp_w1[i], mlp_w2[i])
    return _rmsnorm(x_lat, final_g)


import jax as _jax
import jax.numpy as _jnp

TWIN_FORMAT = 'train_step'
FWD_PARAMS = ['x', 'c', 'ctx', 'c_ctx', 'ada_w', 'ada_b', 'norm_g', 'mlp_w1', 'mlp_w2', 'final_g', 'ssd_w_in', 'ssd_conv_w', 'ssd_conv_b', 'ssd_dt_bias', 'ssd_a_log', 'ssd_d', 'ssd_norm_g', 'ssd_w_out', 'ret_w_in', 'ret_log_decay', 'ret_w_out', 'hgrn_w_in', 'hgrn_lb_logits', 'hgrn_norm_g', 'hgrn_w_out', 'gdn_w_in', 'gdn_conv_w', 'gdn_dt_bias', 'gdn_a_log', 'gdn_norm_g', 'gdn_w_out']
TWIN_WEIGHTS = ['c_ctx', 'ada_w', 'ada_b', 'norm_g', 'mlp_w1', 'mlp_w2', 'final_g', 'ssd_w_in', 'ssd_conv_w', 'ssd_conv_b', 'ssd_dt_bias', 'ssd_a_log', 'ssd_d', 'ssd_norm_g', 'ssd_w_out', 'ret_w_in', 'ret_log_decay', 'ret_w_out', 'hgrn_w_in', 'hgrn_lb_logits', 'hgrn_norm_g', 'hgrn_w_out', 'gdn_w_in', 'gdn_conv_w', 'gdn_dt_bias', 'gdn_a_log', 'gdn_norm_g', 'gdn_w_out']
TWIN_DIFF_INPUT = 'x'
TWIN_INPUTS = ['x', 'c', 'ctx', 'c_ctx', 'ada_w', 'ada_b', 'norm_g', 'mlp_w1', 'mlp_w2', 'final_g', 'ssd_w_in', 'ssd_conv_w', 'ssd_conv_b', 'ssd_dt_bias', 'ssd_a_log', 'ssd_d', 'ssd_norm_g', 'ssd_w_out', 'ret_w_in', 'ret_log_decay', 'ret_w_out', 'hgrn_w_in', 'hgrn_lb_logits', 'hgrn_norm_g', 'hgrn_w_out', 'gdn_w_in', 'gdn_conv_w', 'gdn_dt_bias', 'gdn_a_log', 'gdn_norm_g', 'gdn_w_out', 'loss_target', 'm_c_ctx', 'm_ada_w', 'm_ada_b', 'm_norm_g', 'm_mlp_w1', 'm_mlp_w2', 'm_final_g', 'm_ssd_w_in', 'm_ssd_conv_w', 'm_ssd_conv_b', 'm_ssd_dt_bias', 'm_ssd_a_log', 'm_ssd_d', 'm_ssd_norm_g', 'm_ssd_w_out', 'm_ret_w_in', 'm_ret_log_decay', 'm_ret_w_out', 'm_hgrn_w_in', 'm_hgrn_lb_logits', 'm_hgrn_norm_g', 'm_hgrn_w_out', 'm_gdn_w_in', 'm_gdn_conv_w', 'm_gdn_dt_bias', 'm_gdn_a_log', 'm_gdn_norm_g', 'm_gdn_w_out', 'v_c_ctx', 'v_ada_w', 'v_ada_b', 'v_norm_g', 'v_mlp_w1', 'v_mlp_w2', 'v_final_g', 'v_ssd_w_in', 'v_ssd_conv_w', 'v_ssd_conv_b', 'v_ssd_dt_bias', 'v_ssd_a_log', 'v_ssd_d', 'v_ssd_norm_g', 'v_ssd_w_out', 'v_ret_w_in', 'v_ret_log_decay', 'v_ret_w_out', 'v_hgrn_w_in', 'v_hgrn_lb_logits', 'v_hgrn_norm_g', 'v_hgrn_w_out', 'v_gdn_w_in', 'v_gdn_conv_w', 'v_gdn_dt_bias', 'v_gdn_a_log', 'v_gdn_norm_g', 'v_gdn_w_out']
TWIN_OUTPUTS = ['loss', 'grad_x', 'grad_c_ctx', 'grad_ada_w', 'grad_ada_b', 'grad_norm_g', 'grad_mlp_w1', 'grad_mlp_w2', 'grad_final_g', 'grad_ssd_w_in', 'grad_ssd_conv_w', 'grad_ssd_conv_b', 'grad_ssd_dt_bias', 'grad_ssd_a_log', 'grad_ssd_d', 'grad_ssd_norm_g', 'grad_ssd_w_out', 'grad_ret_w_in', 'grad_ret_log_decay', 'grad_ret_w_out', 'grad_hgrn_w_in', 'grad_hgrn_lb_logits', 'grad_hgrn_norm_g', 'grad_hgrn_w_out', 'grad_gdn_w_in', 'grad_gdn_conv_w', 'grad_gdn_dt_bias', 'grad_gdn_a_log', 'grad_gdn_norm_g', 'grad_gdn_w_out', 'delta_c_ctx', 'delta_ada_w', 'delta_ada_b', 'delta_norm_g', 'delta_mlp_w1', 'delta_mlp_w2', 'delta_final_g', 'delta_ssd_w_in', 'delta_ssd_conv_w', 'delta_ssd_conv_b', 'delta_ssd_dt_bias', 'delta_ssd_a_log', 'delta_ssd_d', 'delta_ssd_norm_g', 'delta_ssd_w_out', 'delta_ret_w_in', 'delta_ret_log_decay', 'delta_ret_w_out', 'delta_hgrn_w_in', 'delta_hgrn_lb_logits', 'delta_hgrn_norm_g', 'delta_hgrn_w_out', 'delta_gdn_w_in', 'delta_gdn_conv_w', 'delta_gdn_dt_bias', 'delta_gdn_a_log', 'delta_gdn_norm_g', 'delta_gdn_w_out', 'new_m_c_ctx', 'new_m_ada_w', 'new_m_ada_b', 'new_m_norm_g', 'new_m_mlp_w1', 'new_m_mlp_w2', 'new_m_final_g', 'new_m_ssd_w_in', 'new_m_ssd_conv_w', 'new_m_ssd_conv_b', 'new_m_ssd_dt_bias', 'new_m_ssd_a_log', 'new_m_ssd_d', 'new_m_ssd_norm_g', 'new_m_ssd_w_out', 'new_m_ret_w_in', 'new_m_ret_log_decay', 'new_m_ret_w_out', 'new_m_hgrn_w_in', 'new_m_hgrn_lb_logits', 'new_m_hgrn_norm_g', 'new_m_hgrn_w_out', 'new_m_gdn_w_in', 'new_m_gdn_conv_w', 'new_m_gdn_dt_bias', 'new_m_gdn_a_log', 'new_m_gdn_norm_g', 'new_m_gdn_w_out', 'new_v_c_ctx', 'new_v_ada_w', 'new_v_ada_b', 'new_v_norm_g', 'new_v_mlp_w1', 'new_v_mlp_w2', 'new_v_final_g', 'new_v_ssd_w_in', 'new_v_ssd_conv_w', 'new_v_ssd_conv_b', 'new_v_ssd_dt_bias', 'new_v_ssd_a_log', 'new_v_ssd_d', 'new_v_ssd_norm_g', 'new_v_ssd_w_out', 'new_v_ret_w_in', 'new_v_ret_log_decay', 'new_v_ret_w_out', 'new_v_hgrn_w_in', 'new_v_hgrn_lb_logits', 'new_v_hgrn_norm_g', 'new_v_hgrn_w_out', 'new_v_gdn_w_in', 'new_v_gdn_conv_w', 'new_v_gdn_dt_bias', 'new_v_gdn_a_log', 'new_v_gdn_norm_g', 'new_v_gdn_w_out']
TWIN_LEAF_KINDS = {'loss': 'loss', 'grad_x': 'grad_x', 'grad_c_ctx': 'grad_w', 'grad_ada_w': 'grad_w', 'grad_ada_b': 'grad_w', 'grad_norm_g': 'grad_w', 'grad_mlp_w1': 'grad_w', 'grad_mlp_w2': 'grad_w', 'grad_final_g': 'grad_w', 'grad_ssd_w_in': 'grad_w', 'grad_ssd_conv_w': 'grad_w', 'grad_ssd_conv_b': 'grad_w', 'grad_ssd_dt_bias': 'grad_w', 'grad_ssd_a_log': 'grad_w', 'grad_ssd_d': 'grad_w', 'grad_ssd_norm_g': 'grad_w', 'grad_ssd_w_out': 'grad_w', 'grad_ret_w_in': 'grad_w', 'grad_ret_log_decay': 'grad_w', 'grad_ret_w_out': 'grad_w', 'grad_hgrn_w_in': 'grad_w', 'grad_hgrn_lb_logits': 'grad_w', 'grad_hgrn_norm_g': 'grad_w', 'grad_hgrn_w_out': 'grad_w', 'grad_gdn_w_in': 'grad_w', 'grad_gdn_conv_w': 'grad_w', 'grad_gdn_dt_bias': 'grad_w', 'grad_gdn_a_log': 'grad_w', 'grad_gdn_norm_g': 'grad_w', 'grad_gdn_w_out': 'grad_w', 'delta_c_ctx': 'delta_w', 'delta_ada_w': 'delta_w', 'delta_ada_b': 'delta_w', 'delta_norm_g': 'delta_w', 'delta_mlp_w1': 'delta_w', 'delta_mlp_w2': 'delta_w', 'delta_final_g': 'delta_w', 'delta_ssd_w_in': 'delta_w', 'delta_ssd_conv_w': 'delta_w', 'delta_ssd_conv_b': 'delta_w', 'delta_ssd_dt_bias': 'delta_w', 'delta_ssd_a_log': 'delta_w', 'delta_ssd_d': 'delta_w', 'delta_ssd_norm_g': 'delta_w', 'delta_ssd_w_out': 'delta_w', 'delta_ret_w_in': 'delta_w', 'delta_ret_log_decay': 'delta_w', 'delta_ret_w_out': 'delta_w', 'delta_hgrn_w_in': 'delta_w', 'delta_hgrn_lb_logits': 'delta_w', 'delta_hgrn_norm_g': 'delta_w', 'delta_hgrn_w_out': 'delta_w', 'delta_gdn_w_in': 'delta_w', 'delta_gdn_conv_w': 'delta_w', 'delta_gdn_dt_bias': 'delta_w', 'delta_gdn_a_log': 'delta_w', 'delta_gdn_norm_g': 'delta_w', 'delta_gdn_w_out': 'delta_w', 'new_m_c_ctx': 'new_m', 'new_m_ada_w': 'new_m', 'new_m_ada_b': 'new_m', 'new_m_norm_g': 'new_m', 'new_m_mlp_w1': 'new_m', 'new_m_mlp_w2': 'new_m', 'new_m_final_g': 'new_m', 'new_m_ssd_w_in': 'new_m', 'new_m_ssd_conv_w': 'new_m', 'new_m_ssd_conv_b': 'new_m', 'new_m_ssd_dt_bias': 'new_m', 'new_m_ssd_a_log': 'new_m', 'new_m_ssd_d': 'new_m', 'new_m_ssd_norm_g': 'new_m', 'new_m_ssd_w_out': 'new_m', 'new_m_ret_w_in': 'new_m', 'new_m_ret_log_decay': 'new_m', 'new_m_ret_w_out': 'new_m', 'new_m_hgrn_w_in': 'new_m', 'new_m_hgrn_lb_logits': 'new_m', 'new_m_hgrn_norm_g': 'new_m', 'new_m_hgrn_w_out': 'new_m', 'new_m_gdn_w_in': 'new_m', 'new_m_gdn_conv_w': 'new_m', 'new_m_gdn_dt_bias': 'new_m', 'new_m_gdn_a_log': 'new_m', 'new_m_gdn_norm_g': 'new_m', 'new_m_gdn_w_out': 'new_m', 'new_v_c_ctx': 'new_v', 'new_v_ada_w': 'new_v', 'new_v_ada_b': 'new_v', 'new_v_norm_g': 'new_v', 'new_v_mlp_w1': 'new_v', 'new_v_mlp_w2': 'new_v', 'new_v_final_g': 'new_v', 'new_v_ssd_w_in': 'new_v', 'new_v_ssd_conv_w': 'new_v', 'new_v_ssd_conv_b': 'new_v', 'new_v_ssd_dt_bias': 'new_v', 'new_v_ssd_a_log': 'new_v', 'new_v_ssd_d': 'new_v', 'new_v_ssd_norm_g': 'new_v', 'new_v_ssd_w_out': 'new_v', 'new_v_ret_w_in': 'new_v', 'new_v_ret_log_decay': 'new_v', 'new_v_ret_w_out': 'new_v', 'new_v_hgrn_w_in': 'new_v', 'new_v_hgrn_lb_logits': 'new_v', 'new_v_hgrn_norm_g': 'new_v', 'new_v_hgrn_w_out': 'new_v', 'new_v_gdn_w_in': 'new_v', 'new_v_gdn_conv_w': 'new_v', 'new_v_gdn_dt_bias': 'new_v', 'new_v_gdn_a_log': 'new_v', 'new_v_gdn_norm_g': 'new_v', 'new_v_gdn_w_out': 'new_v'}


def _forward(args):
    return _fwd_reference(*[args[k] for k in FWD_PARAMS])


def _output_shape():
    out = _jax.eval_shape(lambda: _forward(_fwd_setup_inputs(0)))
    return out.shape, out.dtype

N_MICROBATCH = 1
ADAM_LR = 0.001
ADAM_B1 = 0.9
ADAM_B2 = 0.999
ADAM_EPS = 1e-08
ADAM_WD = 0.01
ADAM_STEP = 10
PER_EXAMPLE_BATCH_AXIS = {'x': 0, 'c': 0, 'ctx': 0, 'loss_target': 0}
SHARED_INPUTS = []
_WEIGHT_DTYPES = {'c_ctx': _jnp.float32, 'ada_w': _jnp.float32, 'ada_b': _jnp.float32, 'norm_g': _jnp.float32, 'mlp_w1': _jnp.float32, 'mlp_w2': _jnp.float32, 'final_g': _jnp.float32, 'ssd_w_in': _jnp.float32, 'ssd_conv_w': _jnp.float32, 'ssd_conv_b': _jnp.float32, 'ssd_dt_bias': _jnp.float32, 'ssd_a_log': _jnp.float32, 'ssd_d': _jnp.float32, 'ssd_norm_g': _jnp.float32, 'ssd_w_out': _jnp.float32, 'ret_w_in': _jnp.float32, 'ret_log_decay': _jnp.float32, 'ret_w_out': _jnp.float32, 'hgrn_w_in': _jnp.float32, 'hgrn_lb_logits': _jnp.float32, 'hgrn_norm_g': _jnp.float32, 'hgrn_w_out': _jnp.float32, 'gdn_w_in': _jnp.float32, 'gdn_conv_w': _jnp.float32, 'gdn_dt_bias': _jnp.float32, 'gdn_a_log': _jnp.float32, 'gdn_norm_g': _jnp.float32, 'gdn_w_out': _jnp.float32}
MOMENT_SCALE = {'c_ctx': 1.196477e-02, 'ada_w': 2.794223e-02, 'ada_b': 4.773367e-02, 'norm_g': 2.491142e-02, 'mlp_w1': 1.355449e-02, 'mlp_w2': 2.440782e-02, 'final_g': 8.096885e+00, 'ssd_w_in': 1.427026e-02, 'ssd_conv_w': 1.319950e-02, 'ssd_conv_b': 1.721369e-02, 'ssd_dt_bias': 2.955965e-02, 'ssd_a_log': 3.922751e-02, 'ssd_d': 6.814184e-02, 'ssd_norm_g': 1.511980e-02, 'ssd_w_out': 2.139869e-02, 'ret_w_in': 1.137110e-02, 'ret_log_decay': 2.267824e+01, 'ret_w_out': 1.205596e-02, 'hgrn_w_in': 1.145339e-02, 'hgrn_lb_logits': 3.281306e-03, 'hgrn_norm_g': 1.060687e-02, 'hgrn_w_out': 1.046229e-02, 'gdn_w_in': 6.414469e-03, 'gdn_conv_w': 6.400155e-03, 'gdn_dt_bias': 1.897015e-02, 'gdn_a_log': 1.928821e-02, 'gdn_norm_g': 4.611196e-02, 'gdn_w_out': 9.671527e-03}


def _to_microbatches(a, axis):
    t = _jnp.moveaxis(a, axis, 0)
    t = t.reshape((N_MICROBATCH, t.shape[0] // N_MICROBATCH) + t.shape[1:])
    return _jnp.moveaxis(t, 1, axis + 1)


def setup_inputs(seed: int = 0) -> dict:
    inp = _fwd_setup_inputs(seed)
    key = _jax.random.fold_in(_jax.random.key(seed), 7919)
    shape, _ = _output_shape()
    out = dict(inp)
    out["loss_target"] = _jax.random.normal(_jax.random.fold_in(key, 0), shape, _jnp.float32)
    for i, name in enumerate(TWIN_WEIGHTS):
        w = inp[name].astype(_jnp.float32)
        if MOMENT_SCALE is None:
            s = _jnp.sqrt(_jnp.mean(_jnp.square(w)) + 1e-30)
        else:
            s = MOMENT_SCALE[name]
        km, kv = _jax.random.split(_jax.random.fold_in(key, i + 1))
        out[name] = w
        out["m_" + name] = s * _jax.random.normal(km, w.shape, _jnp.float32)
        out["v_" + name] = (s * s) * _jax.random.uniform(kv, w.shape, _jnp.float32, 0.5, 1.5)
    if N_MICROBATCH > 1:
        for name, axis in PER_EXAMPLE_BATCH_AXIS.items():
            out[name] = _to_microbatches(out[name], axis)
    return {'x': out['x'], 'c': out['c'], 'ctx': out['ctx'], 'c_ctx': out['c_ctx'], 'ada_w': out['ada_w'], 'ada_b': out['ada_b'], 'norm_g': out['norm_g'], 'mlp_w1': out['mlp_w1'], 'mlp_w2': out['mlp_w2'], 'final_g': out['final_g'], 'ssd_w_in': out['ssd_w_in'], 'ssd_conv_w': out['ssd_conv_w'], 'ssd_conv_b': out['ssd_conv_b'], 'ssd_dt_bias': out['ssd_dt_bias'], 'ssd_a_log': out['ssd_a_log'], 'ssd_d': out['ssd_d'], 'ssd_norm_g': out['ssd_norm_g'], 'ssd_w_out': out['ssd_w_out'], 'ret_w_in': out['ret_w_in'], 'ret_log_decay': out['ret_log_decay'], 'ret_w_out': out['ret_w_out'], 'hgrn_w_in': out['hgrn_w_in'], 'hgrn_lb_logits': out['hgrn_lb_logits'], 'hgrn_norm_g': out['hgrn_norm_g'], 'hgrn_w_out': out['hgrn_w_out'], 'gdn_w_in': out['gdn_w_in'], 'gdn_conv_w': out['gdn_conv_w'], 'gdn_dt_bias': out['gdn_dt_bias'], 'gdn_a_log': out['gdn_a_log'], 'gdn_norm_g': out['gdn_norm_g'], 'gdn_w_out': out['gdn_w_out'], 'loss_target': out['loss_target'], 'm_c_ctx': out['m_c_ctx'], 'm_ada_w': out['m_ada_w'], 'm_ada_b': out['m_ada_b'], 'm_norm_g': out['m_norm_g'], 'm_mlp_w1': out['m_mlp_w1'], 'm_mlp_w2': out['m_mlp_w2'], 'm_final_g': out['m_final_g'], 'm_ssd_w_in': out['m_ssd_w_in'], 'm_ssd_conv_w': out['m_ssd_conv_w'], 'm_ssd_conv_b': out['m_ssd_conv_b'], 'm_ssd_dt_bias': out['m_ssd_dt_bias'], 'm_ssd_a_log': out['m_ssd_a_log'], 'm_ssd_d': out['m_ssd_d'], 'm_ssd_norm_g': out['m_ssd_norm_g'], 'm_ssd_w_out': out['m_ssd_w_out'], 'm_ret_w_in': out['m_ret_w_in'], 'm_ret_log_decay': out['m_ret_log_decay'], 'm_ret_w_out': out['m_ret_w_out'], 'm_hgrn_w_in': out['m_hgrn_w_in'], 'm_hgrn_lb_logits': out['m_hgrn_lb_logits'], 'm_hgrn_norm_g': out['m_hgrn_norm_g'], 'm_hgrn_w_out': out['m_hgrn_w_out'], 'm_gdn_w_in': out['m_gdn_w_in'], 'm_gdn_conv_w': out['m_gdn_conv_w'], 'm_gdn_dt_bias': out['m_gdn_dt_bias'], 'm_gdn_a_log': out['m_gdn_a_log'], 'm_gdn_norm_g': out['m_gdn_norm_g'], 'm_gdn_w_out': out['m_gdn_w_out'], 'v_c_ctx': out['v_c_ctx'], 'v_ada_w': out['v_ada_w'], 'v_ada_b': out['v_ada_b'], 'v_norm_g': out['v_norm_g'], 'v_mlp_w1': out['v_mlp_w1'], 'v_mlp_w2': out['v_mlp_w2'], 'v_final_g': out['v_final_g'], 'v_ssd_w_in': out['v_ssd_w_in'], 'v_ssd_conv_w': out['v_ssd_conv_w'], 'v_ssd_conv_b': out['v_ssd_conv_b'], 'v_ssd_dt_bias': out['v_ssd_dt_bias'], 'v_ssd_a_log': out['v_ssd_a_log'], 'v_ssd_d': out['v_ssd_d'], 'v_ssd_norm_g': out['v_ssd_norm_g'], 'v_ssd_w_out': out['v_ssd_w_out'], 'v_ret_w_in': out['v_ret_w_in'], 'v_ret_log_decay': out['v_ret_log_decay'], 'v_ret_w_out': out['v_ret_w_out'], 'v_hgrn_w_in': out['v_hgrn_w_in'], 'v_hgrn_lb_logits': out['v_hgrn_lb_logits'], 'v_hgrn_norm_g': out['v_hgrn_norm_g'], 'v_hgrn_w_out': out['v_hgrn_w_out'], 'v_gdn_w_in': out['v_gdn_w_in'], 'v_gdn_conv_w': out['v_gdn_conv_w'], 'v_gdn_dt_bias': out['v_gdn_dt_bias'], 'v_gdn_a_log': out['v_gdn_a_log'], 'v_gdn_norm_g': out['v_gdn_norm_g'], 'v_gdn_w_out': out['v_gdn_w_out']}


def _loss(weights, diff, rest, loss_target):
    with _jax.named_scope("forward"):
        args = {**rest, TWIN_DIFF_INPUT: diff, **{k: w.astype(_WEIGHT_DTYPES[k]) for k, w in weights.items()}}
        y = _forward(args)
    with _jax.named_scope("loss_head"):
        err = _jnp.square(y.astype(_jnp.float32) - loss_target)
        return 0.5 * _jnp.sum(_jnp.mean(err, axis=-1)) if err.ndim else 0.5 * err


def _adamw(w, g, m, v):
    m = ADAM_B1 * m + (1.0 - ADAM_B1) * g
    v = ADAM_B2 * v + (1.0 - ADAM_B2) * _jnp.square(g)
    m_hat = m / (1.0 - ADAM_B1 ** ADAM_STEP)
    v_hat = v / (1.0 - ADAM_B2 ** ADAM_STEP)
    delta = -ADAM_LR * (m_hat / (_jnp.sqrt(v_hat) + ADAM_EPS) + ADAM_WD * w)
    return delta, m, v


def reference(x, c, ctx, c_ctx, ada_w, ada_b, norm_g, mlp_w1, mlp_w2, final_g, ssd_w_in, ssd_conv_w, ssd_conv_b, ssd_dt_bias, ssd_a_log, ssd_d, ssd_norm_g, ssd_w_out, ret_w_in, ret_log_decay, ret_w_out, hgrn_w_in, hgrn_lb_logits, hgrn_norm_g, hgrn_w_out, gdn_w_in, gdn_conv_w, gdn_dt_bias, gdn_a_log, gdn_norm_g, gdn_w_out, loss_target, m_c_ctx, m_ada_w, m_ada_b, m_norm_g, m_mlp_w1, m_mlp_w2, m_final_g, m_ssd_w_in, m_ssd_conv_w, m_ssd_conv_b, m_ssd_dt_bias, m_ssd_a_log, m_ssd_d, m_ssd_norm_g, m_ssd_w_out, m_ret_w_in, m_ret_log_decay, m_ret_w_out, m_hgrn_w_in, m_hgrn_lb_logits, m_hgrn_norm_g, m_hgrn_w_out, m_gdn_w_in, m_gdn_conv_w, m_gdn_dt_bias, m_gdn_a_log, m_gdn_norm_g, m_gdn_w_out, v_c_ctx, v_ada_w, v_ada_b, v_norm_g, v_mlp_w1, v_mlp_w2, v_final_g, v_ssd_w_in, v_ssd_conv_w, v_ssd_conv_b, v_ssd_dt_bias, v_ssd_a_log, v_ssd_d, v_ssd_norm_g, v_ssd_w_out, v_ret_w_in, v_ret_log_decay, v_ret_w_out, v_hgrn_w_in, v_hgrn_lb_logits, v_hgrn_norm_g, v_hgrn_w_out, v_gdn_w_in, v_gdn_conv_w, v_gdn_dt_bias, v_gdn_a_log, v_gdn_norm_g, v_gdn_w_out):
    given = dict(x=x, c=c, ctx=ctx, c_ctx=c_ctx, ada_w=ada_w, ada_b=ada_b, norm_g=norm_g, mlp_w1=mlp_w1, mlp_w2=mlp_w2, final_g=final_g, ssd_w_in=ssd_w_in, ssd_conv_w=ssd_conv_w, ssd_conv_b=ssd_conv_b, ssd_dt_bias=ssd_dt_bias, ssd_a_log=ssd_a_log, ssd_d=ssd_d, ssd_norm_g=ssd_norm_g, ssd_w_out=ssd_w_out, ret_w_in=ret_w_in, ret_log_decay=ret_log_decay, ret_w_out=ret_w_out, hgrn_w_in=hgrn_w_in, hgrn_lb_logits=hgrn_lb_logits, hgrn_norm_g=hgrn_norm_g, hgrn_w_out=hgrn_w_out, gdn_w_in=gdn_w_in, gdn_conv_w=gdn_conv_w, gdn_dt_bias=gdn_dt_bias, gdn_a_log=gdn_a_log, gdn_norm_g=gdn_norm_g, gdn_w_out=gdn_w_out, loss_target=loss_target, m_c_ctx=m_c_ctx, m_ada_w=m_ada_w, m_ada_b=m_ada_b, m_norm_g=m_norm_g, m_mlp_w1=m_mlp_w1, m_mlp_w2=m_mlp_w2, m_final_g=m_final_g, m_ssd_w_in=m_ssd_w_in, m_ssd_conv_w=m_ssd_conv_w, m_ssd_conv_b=m_ssd_conv_b, m_ssd_dt_bias=m_ssd_dt_bias, m_ssd_a_log=m_ssd_a_log, m_ssd_d=m_ssd_d, m_ssd_norm_g=m_ssd_norm_g, m_ssd_w_out=m_ssd_w_out, m_ret_w_in=m_ret_w_in, m_ret_log_decay=m_ret_log_decay, m_ret_w_out=m_ret_w_out, m_hgrn_w_in=m_hgrn_w_in, m_hgrn_lb_logits=m_hgrn_lb_logits, m_hgrn_norm_g=m_hgrn_norm_g, m_hgrn_w_out=m_hgrn_w_out, m_gdn_w_in=m_gdn_w_in, m_gdn_conv_w=m_gdn_conv_w, m_gdn_dt_bias=m_gdn_dt_bias, m_gdn_a_log=m_gdn_a_log, m_gdn_norm_g=m_gdn_norm_g, m_gdn_w_out=m_gdn_w_out, v_c_ctx=v_c_ctx, v_ada_w=v_ada_w, v_ada_b=v_ada_b, v_norm_g=v_norm_g, v_mlp_w1=v_mlp_w1, v_mlp_w2=v_mlp_w2, v_final_g=v_final_g, v_ssd_w_in=v_ssd_w_in, v_ssd_conv_w=v_ssd_conv_w, v_ssd_conv_b=v_ssd_conv_b, v_ssd_dt_bias=v_ssd_dt_bias, v_ssd_a_log=v_ssd_a_log, v_ssd_d=v_ssd_d, v_ssd_norm_g=v_ssd_norm_g, v_ssd_w_out=v_ssd_w_out, v_ret_w_in=v_ret_w_in, v_ret_log_decay=v_ret_log_decay, v_ret_w_out=v_ret_w_out, v_hgrn_w_in=v_hgrn_w_in, v_hgrn_lb_logits=v_hgrn_lb_logits, v_hgrn_norm_g=v_hgrn_norm_g, v_hgrn_w_out=v_hgrn_w_out, v_gdn_w_in=v_gdn_w_in, v_gdn_conv_w=v_gdn_conv_w, v_gdn_dt_bias=v_gdn_dt_bias, v_gdn_a_log=v_gdn_a_log, v_gdn_norm_g=v_gdn_norm_g, v_gdn_w_out=v_gdn_w_out)
    weights = {n: given[n] for n in TWIN_WEIGHTS}
    shared = {n: given[n] for n in SHARED_INPUTS}
    per_example = {n: given[n] for n in ['x', 'c', 'ctx']}
    grad_fn = _jax.value_and_grad(_loss, argnums=(0, 1))

    def one_microbatch(ex, loss_target):
        ex = dict(ex)
        diff = ex.pop(TWIN_DIFF_INPUT)
        return grad_fn(weights, diff, {**shared, **ex}, loss_target)

    if N_MICROBATCH == 1:
        loss, (grad_w, grad_x) = one_microbatch(per_example, given["loss_target"])
    else:
        def body(carry, xs):
            loss_sum, grad_sum = carry
            l_k, (gw_k, gx_k) = one_microbatch(xs[0], xs[1])
            with _jax.named_scope("update"):
                return (loss_sum + l_k, _jax.tree.map(_jnp.add, grad_sum, gw_k)), gx_k

        init = (_jnp.zeros((), _jnp.float32), _jax.tree.map(_jnp.zeros_like, weights))
        (loss, grad_w), grad_x = _jax.lax.scan(body, init, (per_example, given["loss_target"]))
    with _jax.named_scope("update"):
        delta_w, new_m, new_v = {}, {}, {}
        for n in TWIN_WEIGHTS:
            delta_w[n], new_m[n], new_v[n] = _adamw(weights[n], grad_w[n], given["m_" + n], given["v_" + n])
    return (loss, grad_x, *[grad_w[n] for n in TWIN_WEIGHTS], *[delta_w[n] for n in TWIN_WEIGHTS],
            *[new_m[n] for n in TWIN_WEIGHTS], *[new_v[n] for n in TWIN_WEIGHTS])
```

```python
import functools
import math

import jax
import jax.numpy as jnp
from jax import lax
from jax.experimental import pallas as pl
from jax.experimental.pallas import tpu as pltpu

F32 = jnp.float32
BF16 = jnp.bfloat16
HI = lax.Precision.HIGHEST

N_DEV = 8
CHUNK = 64
SUB = 16
GRID_W = 64
ROPE_BASE = 10000.0
NORM_EPS = 1e-6
DEPTH = 4
SSD_GROUPS = 8
SSD_STATE = 128
SSD_HEAD_DIM = 64
RET_HEADS = 8
HEAD128 = 128
ADAM_LR, ADAM_B1, ADAM_B2, ADAM_EPS, ADAM_WD, ADAM_STEP = 0.001, 0.9, 0.999, 1e-08, 0.01, 10

V7X_VMEM_BYTES = 64 * 1024 * 1024
VMEM_LIMIT = (V7X_VMEM_BYTES * 3) // 4
ROW_PIPELINE_BYTES = V7X_VMEM_BYTES // 4
LANES = 128
FLAT_ALIGN_ROWS = 1024
MESH = pl.DeviceIdType.MESH

SDS = jax.ShapeDtypeStruct


def _cp(*sem):
    return pltpu.CompilerParams(dimension_semantics=tuple(sem), vmem_limit_bytes=VMEM_LIMIT)


def _pick(n, cap, quantum=LANES):
    best = None
    d = quantum
    while d <= min(n, cap):
        if n % d == 0:
            best = d
        d += quantum
    return n if best is None else best


def _mm(a, b, mode, name, out_dtype=F32):
    if mode == "nn":
        (m, k), n = a.shape, b.shape[1]
    elif mode == "nt":
        (m, k), n = a.shape, b.shape[0]
    else:
        (k, m), n = a.shape, b.shape[1]
    tm, tn, tk = _pick(m, 1024), _pick(n, 1280), _pick(k, 512)
    nk = k // tk
    if mode == "nn":
        a_spec = pl.BlockSpec((tm, tk), lambda i, j, kk: (i, kk))
        b_spec = pl.BlockSpec((tk, tn), lambda i, j, kk: (kk, j))
        dims = (((1,), (0,)), ((), ()))
    elif mode == "nt":
        a_spec = pl.BlockSpec((tm, tk), lambda i, j, kk: (i, kk))
        b_spec = pl.BlockSpec((tn, tk), lambda i, j, kk: (j, kk))
        dims = (((1,), (1,)), ((), ()))
    else:
        a_spec = pl.BlockSpec((tk, tm), lambda i, j, kk: (kk, i))
        b_spec = pl.BlockSpec((tk, tn), lambda i, j, kk: (kk, j))
        dims = (((0,), (0,)), ((), ()))

    def body(a_ref, b_ref, o_ref, acc_ref):
        kk = pl.program_id(2)

        @pl.when(kk == 0)
        def _():
            acc_ref[...] = jnp.zeros_like(acc_ref)

        acc_ref[...] += lax.dot_general(a_ref[...].astype(BF16), b_ref[...].astype(BF16), dims,
                                        preferred_element_type=F32)

        @pl.when(kk == nk - 1)
        def _():
            o_ref[...] = acc_ref[...].astype(o_ref.dtype)

    return pl.pallas_call(
        body, grid=(m // tm, n // tn, nk), in_specs=[a_spec, b_spec],
        out_specs=pl.BlockSpec((tm, tn), lambda i, j, kk: (i, j)),
        out_shape=SDS((m, n), out_dtype), scratch_shapes=[pltpu.VMEM((tm, tn), F32)],
        compiler_params=_cp("parallel", "parallel", "arbitrary"), name=name)(a, b)


def pmatmul(a, w, name):
    @jax.custom_vjp
    def op(a, w):
        return _mm(a, w, "nn", name + "_fwd")

    def fwd(a, w):
        return _mm(a, w, "nn", name + "_fwd"), (a, w)

    def bwd(res, g):
        a, w = res
        return _mm(g, w, "nt", name + "_dx"), _mm(a, g, "tn", name + "_dw", out_dtype=w.dtype)

    op.defvjp(fwd, bwd)
    return op(a, w)


def rowop(f, name, rows, pars, out_ws, tile, ctx_rows=0):
    t = rows[0].shape[0]
    tile = math.gcd(math.gcd(tile, t), ctx_rows if ctx_rows else t)
    row_bytes = 2 * 4 * (2 * sum(r.shape[1] for r in rows) + sum(out_ws))
    while tile > 8 and tile % 2 == 0 and tile * row_bytes > ROW_PIPELINE_BYTES:
        tile //= 2
    nt = t // tile
    cut = ctx_rows // tile
    n_r, n_p, n_o = len(rows), len(pars), len(out_ws)
    segs = [p.shape[0] for p in pars]
    assert all(s in (1, 2) for s in segs)

    def par_map(s):
        if s == 1:
            return lambda i: (0, 0, 0)
        return lambda i: (jnp.where(i >= cut, 1, 0), 0, 0)

    row_specs = [pl.BlockSpec((tile, r.shape[1]), lambda i: (i, 0)) for r in rows]
    par_specs = [pl.BlockSpec((None, 1, p.shape[2]), par_map(p.shape[0])) for p in pars]
    out_specs = [pl.BlockSpec((tile, w), lambda i: (i, 0)) for w in out_ws]
    out_shape = [SDS((t, w), F32) for w in out_ws]

    def fwd_call(*args):
        def body(*refs):
            outs = f(*[r[...] for r in refs[:n_r + n_p]])
            for o_ref, o in zip(refs[n_r + n_p:], outs):
                o_ref[...] = o

        return tuple(pl.pallas_call(body, grid=(nt,), in_specs=row_specs + par_specs, out_specs=out_specs,
                                    out_shape=out_shape, compiler_params=_cp("parallel"), name=name + "_fwd")(*args))

    def bwd_call(args, gouts):
        def body(*refs):
            ins = [r[...] for r in refs[:n_r + n_p]]
            gs = tuple(r[...] for r in refs[n_r + n_p:n_r + n_p + n_o])
            d_refs = refs[n_r + n_p + n_o:]
            _, vjp = jax.vjp(f, *ins)
            grads = vjp(gs)
            for ref, g in zip(d_refs[:n_r], grads[:n_r]):
                ref[...] = g
            i = pl.program_id(0)
            for ref, g, s in zip(d_refs[n_r:], grads[n_r:], segs):
                first = (i == 0) if s == 1 else jnp.logical_or(i == 0, i == cut)

                @pl.when(first)
                def _(ref=ref, g=g):
                    ref[...] = g

                @pl.when(jnp.logical_not(first))
                def _(ref=ref, g=g):
                    ref[...] += g

        d_specs = row_specs + par_specs
        d_shape = [SDS(r.shape, F32) for r in rows] + [SDS(p.shape, F32) for p in pars]
        return tuple(pl.pallas_call(body, grid=(nt,), in_specs=row_specs + par_specs + out_specs, out_specs=d_specs,
                                    out_shape=d_shape, compiler_params=_cp("arbitrary"), name=name + "_bwd")(*args, *gouts))

    @jax.custom_vjp
    def op(*args):
        return fwd_call(*args)

    op.defvjp(lambda *args: (fwd_call(*args), args), bwd_call)
    return op(*rows, *pars)


def scanop(chunk_fn, name, cols, col_ws, grps, state_shape, y_w):
    t = cols[0].shape[0]
    g_n = cols[0].shape[1] // col_ws[0]
    nc = t // CHUNK
    n_c, n_g = len(cols), len(grps)

    def specs(cmap):
        col_specs = [pl.BlockSpec((CHUNK, w), lambda g, c: (cmap(c), g)) for w in col_ws]
        grp_specs = [pl.BlockSpec((None, CHUNK, a.shape[2]), lambda g, c: (g, cmap(c), 0)) for a in grps]
        y_spec = pl.BlockSpec((CHUNK, y_w), lambda g, c: (cmap(c), g))
        s_spec = pl.BlockSpec((None, None) + state_shape, lambda g, c: (g, cmap(c), 0, 0))
        return col_specs, grp_specs, y_spec, s_spec

    def fwd_call(*args):
        col_specs, grp_specs, y_spec, s_spec = specs(lambda c: c)

        def body(*refs):
            ins = [r[...] for r in refs[:n_c + n_g]]
            y_ref, sall_ref, s_ref = refs[n_c + n_g:]

            @pl.when(pl.program_id(1) == 0)
            def _():
                s_ref[...] = jnp.zeros_like(s_ref)

            s = s_ref[...]
            sall_ref[...] = s
            s_new, y = chunk_fn(s, *ins)
            y_ref[...] = y
            s_ref[...] = s_new

        return pl.pallas_call(
            body, grid=(g_n, nc), in_specs=col_specs + grp_specs, out_specs=[y_spec, s_spec],
            out_shape=[SDS((t, g_n * y_w), F32), SDS((g_n, nc) + state_shape, F32)],
            scratch_shapes=[pltpu.VMEM(state_shape, F32)],
            compiler_params=_cp("parallel", "arbitrary"), name=name + "_fwd")(*args)

    def bwd_call(res, gy):
        args, s_all = res
        col_specs, grp_specs, y_spec, s_spec = specs(lambda c: nc - 1 - c)

        def body(*refs):
            ins = [r[...] for r in refs[:n_c + n_g]]
            s_prev = refs[n_c + n_g][...]
            dy = refs[n_c + n_g + 1][...]
            d_refs = refs[n_c + n_g + 2:-1]
            ds_ref = refs[-1]

            @pl.when(pl.program_id(1) == 0)
            def _():
                ds_ref[...] = jnp.zeros_like(ds_ref)

            _, vjp = jax.vjp(chunk_fn, s_prev, *ins)
            grads = vjp((ds_ref[...], dy))
            ds_ref[...] = grads[0]
            for ref, g in zip(d_refs, grads[1:]):
                ref[...] = g

        return tuple(pl.pallas_call(
            body, grid=(g_n, nc), in_specs=col_specs + grp_specs + [s_spec, y_spec], out_specs=col_specs + grp_specs,
            out_shape=[SDS(a.shape, F32) for a in args],
            scratch_shapes=[pltpu.VMEM(state_shape, F32)],
            compiler_params=_cp("parallel", "arbitrary"), name=name + "_bwd")(*args, s_all, gy))

    @jax.custom_vjp
    def op(*args):
        return fwd_call(*args)[0]

    def fwd(*args):
        y, s_all = fwd_call(*args)
        return y, (args, s_all)

    op.defvjp(fwd, bwd_call)
    return op(*cols, *grps)


def _dg(a, b, ca, cb, prec=None):
    return lax.dot_general(a, b, (((ca,), (cb,)), ((), ())), precision=prec, preferred_element_type=F32)


@functools.partial(jax.custom_vjp, nondiff_argnums=(2, 3))
def mmb(a, b, ca, cb):
    return _dg(a.astype(BF16), b.astype(BF16), ca, cb)


def _mmb_fwd(a, b, ca, cb):
    return mmb(a, b, ca, cb), (a, b)


def _mmb_bwd(ca, cb, res, g):
    a, b = res
    ab, bb, gb = a.astype(BF16), b.astype(BF16), g.astype(BF16)
    da = _dg(gb, bb, 1, 1 - cb) if ca == 1 else _dg(bb, gb, 1 - cb, 1)
    db = _dg(ab, gb, 1 - ca, 0) if cb == 0 else _dg(gb, ab, 0, 1 - ca)
    return da, db


mmb.defvjp(_mmb_fwd, _mmb_bwd)


def mmh(a, b, ca=1, cb=0):
    return _dg(a, b, ca, cb, HI)


def _masks(n):
    i = lax.broadcasted_iota(jnp.int32, (n, n), 0)
    j = lax.broadcasted_iota(jnp.int32, (n, n), 1)
    return i >= j, i > j


def _silu(x):
    return x * jax.nn.sigmoid(x)


def _softplus(x):
    return jnp.maximum(x, 0.0) + jnp.log(1.0 + jnp.exp(-jnp.abs(x)))


def _expand_heads(p, width):
    h = p.shape[1]
    lane = lax.broadcasted_iota(jnp.int32, (h, h * width), 1)
    row = lax.broadcasted_iota(jnp.int32, (h, h * width), 0)
    e = jnp.where(lane // width == row, 1.0, 0.0).astype(F32)
    return mmh(jnp.broadcast_to(p, (8, h)), e)[0:1]


def _group_apply(x, width, fn):
    n = x.shape[1] // width
    return jnp.concatenate([fn(x[:, g * width:(g + 1) * width]) for g in range(n)], axis=1)


def _rms(x):
    return x * lax.rsqrt(jnp.mean(x * x, axis=-1, keepdims=True) + NORM_EPS)


@jax.custom_vjp
def unit_lower_inv(l):
    n = l.shape[0]
    eye = jnp.where(_masks(n)[0] & jnp.logical_not(_masks(n)[1]), 1.0, 0.0).astype(F32)
    p = -l
    x = eye + p
    for _ in range(int(math.log2(n)) - 1):
        p = mmh(p, p)
        x = x + mmh(x, p)
    return x


def _uli_fwd(l):
    x = unit_lower_inv(l)
    return x, x


def _uli_bwd(x, g):
    return (-_dg(_dg(x, g, 0, 0, HI), x, 1, 1, HI),)


unit_lower_inv.defvjp(_uli_fwd, _uli_bwd)


def scalar_decay_chunk(r_n, n, p, with_dt):
    def fn(s, q, k, v, la, *rest):
        c = q.shape[0]
        incl, _ = _masks(c)
        tril = incl.astype(F32)
        cum = mmh(tril, la)
        cum_t = _dg(la, tril, 0, 1, HI)
        tot = cum[c - 1:c, :]
        scores = mmb(q, k, 1, 1)
        ys, ss = [], []
        for r in range(r_n):
            cr = cum[:, r:r + 1]
            dec = jnp.exp(jnp.where(incl, cr - cum_t[r:r + 1, :], -jnp.inf))
            vr = v[:, r * p:(r + 1) * p]
            if with_dt:
                vr = vr * rest[0][:, r:r + 1]
            sr = s[r * n:(r + 1) * n, :]
            ys.append(mmb(scores * dec, vr, 1, 0) + mmb(q, sr, 1, 0) * jnp.exp(cr))
            ss.append(jnp.exp(tot[:, r:r + 1]) * sr + mmb(k, jnp.exp(tot[:, r:r + 1] - cr) * vr, 0, 0))
        return jnp.concatenate(ss, axis=0), jnp.concatenate(ys, axis=1)

    return fn


def vector_decay_chunk(s, q, k, v, lf):
    c, kd = q.shape
    incl, _ = _masks(c)
    cum = mmh(incl.astype(F32), lf)
    tot = cum[c - 1:c, :]
    tot_col = _dg(lf, jnp.ones((c, 1), F32), 0, 0, HI)
    y_inter = mmb(q * jnp.exp(cum), s, 1, 0)
    s_new = jnp.exp(tot_col) * s + mmb(k * jnp.exp(tot - cum), v, 0, 0)
    i3 = lax.broadcasted_iota(jnp.int32, (SUB, SUB, 1), 0)
    j3 = lax.broadcasted_iota(jnp.int32, (SUB, SUB, 1), 1)
    ys = []
    for b in range(c // SUB):
        lo, hi = b * SUB, (b + 1) * SUB
        qb, kb, cb, vb = q[lo:hi], k[lo:hi], cum[lo:hi], v[lo:hi]
        dec = jnp.exp(jnp.where(i3 >= j3, cb[:, None, :] - cb[None, :, :], -jnp.inf))
        a_diag = jnp.sum(qb[:, None, :] * kb[None, :, :] * dec, axis=-1)
        yb = mmb(a_diag, vb, 1, 0)
        if b > 0:
            base = cum[lo - 1:lo, :]
            a_off = mmb(qb * jnp.exp(cb - base), k[:lo] * jnp.exp(base - cum[:lo]), 1, 1)
            yb = yb + mmb(a_off, v[:lo], 1, 0)
        ys.append(yb)
    return s_new, y_inter + jnp.concatenate(ys, axis=0)


def delta_chunk(r_n, kd, vd):
    def fn(s, q, k, v, beta, la):
        c = q.shape[0]
        incl, strict = _masks(c)
        tril = incl.astype(F32)
        cum = mmh(tril, la)
        cum_t = _dg(la, tril, 0, 1, HI)
        tot = cum[c - 1:c, :]
        kk = mmb(k, k, 1, 1)
        qk = mmb(q, k, 1, 1)
        ys, ss = [], []
        for r in range(r_n):
            cr, br = cum[:, r:r + 1], beta[:, r:r + 1]
            seg = cr - cum_t[r:r + 1, :]
            x = unit_lower_inv(br * kk * jnp.exp(jnp.where(strict, seg, -jnp.inf)))
            vr = v[:, r * vd:(r + 1) * vd]
            sr = s[r * kd:(r + 1) * kd, :]
            u = mmh(x, vr * br)
            w = mmh(x, k * (br * jnp.exp(cr)))
            v_new = u - mmb(w, sr, 1, 0)
            attn = qk * jnp.exp(jnp.where(incl, seg, -jnp.inf))
            ys.append(mmb(attn, v_new, 1, 0) + mmb(q, sr, 1, 0) * jnp.exp(cr))
            ss.append(jnp.exp(tot[:, r:r + 1]) * sr + mmb(k, jnp.exp(tot[:, r:r + 1] - cr) * v_new, 0, 0))
        return jnp.concatenate(ss, axis=0), jnp.concatenate(ys, axis=1)

    return fn


def f_adaln(x, g, sh, sc):
    return ((_rms(x) * g) * (1.0 + sc) + sh,)


def f_resid(x, y, gate):
    return (x + gate * y,)


def f_relu2(h):
    return (jnp.square(jnp.maximum(h, 0.0)),)


def f_conv_bias(up, u, un, w0, w1, w2, b):
    return (_silu(w0 * up + w1 * u + w2 * un + b),)


def f_conv(up, u, un, w0, w1, w2):
    return (_silu(w0 * up + w1 * u + w2 * un),)


def f_dt(raw, bias, a_log):
    dt = _softplus(raw + bias)
    return dt, -jnp.exp(a_log) * dt


def f_ssd_post(yf, yb, xs, z, d_skip, ng):
    y = (yf + yb + _expand_heads(d_skip, SSD_HEAD_DIM) * xs) * _silu(z)
    return (_group_apply(y, y.shape[1] // SSD_GROUPS, _rms) * ng,)


def f_rope(scale):
    def f(q, qsw, cos, sin):
        return ((q * cos + qsw * sin) * scale,)
    return f


def f_ret_post(yf, yb, g):
    def ln(y):
        mu = jnp.mean(y, axis=-1, keepdims=True)
        var = jnp.mean(jnp.square(y - mu), axis=-1, keepdims=True)
        return (y - mu) * lax.rsqrt(var + NORM_EPS)
    y = yf + yb
    return (_group_apply(y, y.shape[1] // RET_HEADS, ln) * _silu(g),)


def f_lower_bound(layer):
    def f(logits):
        m = jnp.max(logits, axis=0, keepdims=True)
        e = jnp.exp(logits - m)
        p = e / jnp.sum(e, axis=0, keepdims=True)
        lb = p[1:2]
        for j in range(2, layer + 1):
            lb = lb + p[j:j + 1]
        return (jnp.broadcast_to(lb, logits.shape),)
    return f


def f_hgrn_gates(ff, fb, lb):
    def gates(f):
        log_sig = jnp.minimum(f, 0.0) - jnp.log(1.0 + jnp.exp(-jnp.abs(f)))
        a, b = jnp.log(lb), jnp.log(1.0 - lb) + log_sig
        m = jnp.maximum(a, b)
        return m + jnp.log(jnp.exp(a - m) + jnp.exp(b - m)), (1.0 - lb) * jax.nn.sigmoid(-f)
    lf_f, k_f = gates(ff)
    lf_b, k_b = gates(fb)
    return lf_f, k_f, lf_b, k_b


def f_hgrn_post(yf, yb, g, ng):
    return (_group_apply(yf + yb, HEAD128, _rms) * ng * _silu(g),)


def f_gdn_qk(q, k):
    def l2(x):
        return x * lax.rsqrt(jnp.sum(x * x, axis=-1, keepdims=True) + 1e-6)
    return _group_apply(q, HEAD128, l2) * HEAD128 ** -0.5, _group_apply(k, HEAD128, l2)


def f_gdn_gates(bt, a, dt_bias, a_log):
    return jax.nn.sigmoid(bt), -jnp.exp(a_log) * _softplus(a + dt_bias)


def f_gdn_post(yf, yb, z, ng):
    n = yf.shape[1] // HEAD128
    return (_group_apply(yf + yb, HEAD128, _rms) * jnp.concatenate([ng] * n, axis=1) * _silu(z),)


def f_loss(x, tgt, fg):
    err = jnp.square(_rms(x) * fg - tgt)
    return (jnp.broadcast_to(jnp.mean(err, axis=-1, keepdims=True), (x.shape[0], LANES)),)


def _par(v):
    return v.reshape(1, 1, -1)


def _rev(t, lc):
    return jnp.concatenate([jnp.flip(t[:lc], 0), jnp.flip(t[lc:], 0)], axis=0)


def _shift(u, lc):
    z = jnp.zeros((1, u.shape[1]), u.dtype)
    prev = jnp.concatenate([z, u[:lc - 1], z, u[lc:-1]], axis=0)
    nxt = jnp.concatenate([u[1:lc], z, u[lc + 1:], z], axis=0)
    return prev, nxt


def _grp(a, g_n):
    t = a.shape[0]
    return a.reshape(t, g_n, -1).transpose(1, 0, 2)


def _bidir(scan, cols, grps_f, grps_b, lc):
    yf = scan("f", cols, grps_f)
    yb = scan("b", [_rev(a, lc) for a in cols], [jnp.concatenate([jnp.flip(a[:, :lc], 1), jnp.flip(a[:, lc:], 1)], axis=1)
                                                for a in grps_b])
    return yf, _rev(yb, lc)


def ssd_mixer(h, p, lc, start):
    d = h.shape[1]
    di = 2 * d
    gn = SSD_GROUPS * SSD_STATE
    heads = di // SSD_HEAD_DIM
    r_n = heads // SSD_GROUPS
    u = pmatmul(h, p["ssd_w_in"], "ssd_in")
    z, xbc, dtr = u[:, :di], u[:, di:di + di + 2 * gn], u[:, 2 * di + 2 * gn:]
    up, un = _shift(xbc, lc)
    cw = p["ssd_conv_w"]
    (xbc,) = rowop(f_conv_bias, "ssd_conv", [up, xbc, un], [_par(cw[0]), _par(cw[1]), _par(cw[2]), _par(p["ssd_conv_b"])],
                   [xbc.shape[1]], 128)
    xs, bm, cm = xbc[:, :di], xbc[:, di:di + gn], xbc[:, di + gn:]
    dt, la = rowop(f_dt, "ssd_dt", [dtr], [_par(p["ssd_dt_bias"]), _par(p["ssd_a_log"])], [2 * heads, 2 * heads], 256)
    chunk = scalar_decay_chunk(r_n, SSD_STATE, SSD_HEAD_DIM, True)

    def scan(tag, cols, grps):
        return scanop(chunk, "ssd_scan_" + tag, cols, [SSD_STATE, SSD_STATE, r_n * SSD_HEAD_DIM], grps,
                      (r_n * SSD_STATE, SSD_HEAD_DIM), r_n * SSD_HEAD_DIM)

    yf, yb = _bidir(scan, [cm, bm, xs],
                    [_grp(la[:, :heads], SSD_GROUPS), _grp(dt[:, :heads], SSD_GROUPS)],
                    [_grp(la[:, heads:], SSD_GROUPS), _grp(dt[:, heads:], SSD_GROUPS)], lc)
    (yn,) = rowop(f_ssd_post, "ssd_post", [yf[start:], yb[start:], xs[start:], z[start:]],
                  [_par(p["ssd_d"]), _par(p["ssd_norm_g"])], [di], 128)
    return pmatmul(yn, p["ssd_w_out"], "ssd_out")


def _rope_tables(n_lat, lc, d, heads):
    qk = d // heads
    half = qk // 2
    quarter = half // 2
    pos = jnp.arange(n_lat)
    inv_freq = ROPE_BASE ** (-jnp.arange(0, half, 2, dtype=F32) / half)

    def tab(p):
        ang = p.astype(F32)[:, None] * inv_freq
        return jnp.cos(ang), jnp.sin(ang)

    cr, sr = tab(pos // GRID_W)
    cc, sc = tab(pos % GRID_W)
    cos = jnp.concatenate([cr, cr, cc, cc], axis=1)
    sin = jnp.concatenate([-sr, sr, -sc, sc], axis=1)
    cos = jnp.concatenate([jnp.ones((lc, qk), F32), cos], axis=0)
    sin = jnp.concatenate([jnp.zeros((lc, qk), F32), sin], axis=0)
    return jnp.tile(cos, (1, heads)), jnp.tile(sin, (1, heads)), quarter


def retention_mixer(h, p, lc, start):
    t, d = h.shape
    dv = 2 * d
    qk = d // RET_HEADS
    u = pmatmul(h, p["ret_w_in"], "ret_in")
    q, k, v, g = u[:, :d], u[:, d:2 * d], u[:, 2 * d:2 * d + dv], u[:, 2 * d + dv:]
    cos, sin, quarter = _rope_tables(t - lc, lc, d, RET_HEADS)
    cos, sin = lax.stop_gradient(cos), lax.stop_gradient(sin)

    def swap(a):
        return a.reshape(t, -1, 2, quarter)[:, :, ::-1, :].reshape(t, d)

    (q,) = rowop(f_rope(1.0), "ret_rope_q", [q, swap(q), cos, sin], [], [d], 256)
    (k,) = rowop(f_rope(qk ** -0.5), "ret_rope_k", [k, swap(k), cos, sin], [], [d], 256)
    chunk = scalar_decay_chunk(1, qk, dv // RET_HEADS, False)

    def scan(tag, cols, grps):
        return scanop(chunk, "ret_scan_" + tag, cols, [qk, qk, dv // RET_HEADS], grps, (qk, dv // RET_HEADS), dv // RET_HEADS)

    ld = p["ret_log_decay"]
    la_f = jnp.broadcast_to(ld[0][:, None, None], (RET_HEADS, t, 1))
    la_b = jnp.broadcast_to(ld[1][:, None, None], (RET_HEADS, t, 1))
    yf, yb = _bidir(scan, [q, k, v], [la_f], [la_b], lc)
    (yn,) = rowop(f_ret_post, "ret_post", [yf[start:], yb[start:], g[start:]], [], [dv], 128)
    return pmatmul(yn, p["ret_w_out"], "ret_out")


def hgrn2_mixer(h, p, lc, start, layer):
    t, d = h.shape
    u = pmatmul(h, p["hgrn_w_in"], "hgrn_in")
    q, f_f, f_b, i, g = (u[:, j * d:(j + 1) * d] for j in range(5))
    (lb,) = rowop(f_lower_bound(layer), "hgrn_lb", [p["hgrn_lb_logits"]], [], [d], DEPTH)
    lf_f, k_f, lf_b, k_b = rowop(f_hgrn_gates, "hgrn_gates", [f_f, f_b], [lb[0:1][None]], [d] * 4, 256)

    def scan(tag, cols):
        return scanop(vector_decay_chunk, "hgrn_scan_" + tag, cols, [HEAD128] * 4, [], (HEAD128, HEAD128), HEAD128)

    yf = scan("f", [q, k_f, i, lf_f])
    yb = _rev(scan("b", [_rev(a, lc) for a in (q, k_b, i, lf_b)]), lc)
    (yn,) = rowop(f_hgrn_post, "hgrn_post", [yf[start:], yb[start:], g[start:]], [_par(p["hgrn_norm_g"])], [d], 256)
    return pmatmul(yn, p["hgrn_w_out"], "hgrn_out")


def gdn_mixer(h, p, lc, start):
    t, d = h.shape
    dk, dv = d, 2 * d
    kh = d // HEAD128
    r_n = 2
    cc = 2 * dk + dv
    u = pmatmul(h, p["gdn_w_in_main"], "gdn_in")
    ug = pmatmul(h, p["gdn_w_in_gate"], "gdn_in_gate")
    qkv, z = u[:, :cc], u[:, cc:]
    nb = 2 * kh * r_n
    bt, a = ug[:, :nb], ug[:, nb:]
    up, un = _shift(qkv, lc)
    cw = p["gdn_conv_w"]
    (qkv,) = rowop(f_conv, "gdn_conv", [up, qkv, un], [_par(cw[0]), _par(cw[1]), _par(cw[2])], [cc], 128)
    q, k, v = qkv[:, :dk], qkv[:, dk:2 * dk], qkv[:, 2 * dk:]
    q, k = rowop(f_gdn_qk, "gdn_qk", [q, k], [], [dk, dk], 256)
    beta, la = rowop(f_gdn_gates, "gdn_gates", [bt, a], [_par(p["gdn_dt_bias"]), _par(p["gdn_a_log"])], [nb, nb], 256)
    chunk = delta_chunk(r_n, HEAD128, HEAD128)

    def scan(tag, cols, grps):
        return scanop(chunk, "gdn_scan_" + tag, cols, [HEAD128, HEAD128, r_n * HEAD128], grps,
                      (r_n * HEAD128, HEAD128), r_n * HEAD128)

    half = nb // 2
    yf, yb = _bidir(scan, [q, k, v],
                    [_grp(beta[:, :half], kh), _grp(la[:, :half], kh)],
                    [_grp(beta[:, half:], kh), _grp(la[:, half:], kh)], lc)
    (yn,) = rowop(f_gdn_post, "gdn_post", [yf[start:], yb[start:], z[start:]], [_par(p["gdn_norm_g"])], [dv], 128)
    return pmatmul(yn, p["gdn_w_out"], "gdn_out")


def local_loss(x, ctx, mod2, tgt, p):
    lc, d = ctx.shape
    xc = jnp.concatenate([ctx, x], axis=0)
    for i in range(DEPTH):
        keep = i < DEPTH - 1
        start = 0 if keep else lc
        sh1, sc1, g1, sh2, sc2, g2 = (mod2[i][:, j * d:(j + 1) * d][:, None, :] for j in range(6))
        (h,) = rowop(f_adaln, "adaln_a%d" % i, [xc], [_par(p["norm_g"][i, 0]), sh1, sc1], [d], 256, lc)
        if i % 4 == 0:
            y = ssd_mixer(h, p, lc, start)
        elif i % 4 == 1:
            y = retention_mixer(h, p, lc, start)
        elif i % 4 == 2:
            y = hgrn2_mixer(h, p, lc, start, i)
        else:
            y = gdn_mixer(h, p, lc, start)
        if not keep:
            xc, g1, sh2, sc2, g2, lc = xc[lc:], g1[1:], sh2[1:], sc2[1:], g2[1:], 0
        (xc,) = rowop(f_resid, "resid_a%d" % i, [xc, y], [g1], [d], 256, lc)
        (h2,) = rowop(f_adaln, "adaln_b%d" % i, [xc], [_par(p["norm_g"][i, 1]), sh2, sc2], [d], 256, lc)
        a = pmatmul(h2, p["mlp_w1"][i], "mlp1_%d" % i)
        (a,) = rowop(f_relu2, "relu2_%d" % i, [a], [], [a.shape[1]], 128)
        m = pmatmul(a, p["mlp_w2"][i], "mlp2_%d" % i)
        (xc,) = rowop(f_resid, "resid_b%d" % i, [xc, m], [g2], [d], 256, lc)
    (rows,) = rowop(f_loss, "loss", [xc, tgt], [_par(p["final_g"])], [LANES], 256)
    return 0.5 * jnp.sum(rows[:, 0])


def exchange(x, gather, name):
    blk = x.shape if gather else x.shape[1:]

    def body(x_ref, o_ref, send_sems, recv_sems, local_sem):
        mx, my, mc = lax.axis_index("x"), lax.axis_index("y"), lax.axis_index("c")
        me = 4 * mx + 2 * my + mc
        copies = []
        for k in range(1, N_DEV):
            px = 1 - mx if (k >> 2) & 1 else mx
            py = 1 - my if (k >> 1) & 1 else my
            pc = 1 - mc if k & 1 else mc
            src = x_ref if gather else x_ref.at[4 * px + 2 * py + pc]
            cp = pltpu.make_async_remote_copy(src_ref=src, dst_ref=o_ref.at[me], send_sem=send_sems.at[k - 1],
                                              recv_sem=recv_sems.at[k - 1], device_id=(px, py, pc), device_id_type=MESH)
            cp.start()
            copies.append(cp)
        mine = pltpu.make_async_copy(x_ref if gather else x_ref.at[me], o_ref.at[me], local_sem)
        mine.start()
        for cp in copies:
            cp.wait_recv()
        for cp in copies:
            cp.wait_send()
        mine.wait()

    return pl.pallas_call(
        body, out_shape=SDS((N_DEV,) + tuple(blk), x.dtype),
        in_specs=[pl.BlockSpec(memory_space=pl.ANY)], out_specs=pl.BlockSpec(memory_space=pl.ANY),
        scratch_shapes=[pltpu.SemaphoreType.DMA((N_DEV - 1,)), pltpu.SemaphoreType.DMA((N_DEV - 1,)),
                        pltpu.SemaphoreType.DMA],
        name=name)(x)


def sum_parts(parts, name):
    n_p, rows, _ = parts.shape
    tr = _pick(rows, 1024, 8)

    def body(p_ref, o_ref):
        acc = p_ref[0].astype(F32)
        for j in range(1, n_p):
            acc = acc + p_ref[j].astype(F32)
        o_ref[...] = acc

    return pl.pallas_call(body, grid=(rows // tr,), in_specs=[pl.BlockSpec((n_p, tr, LANES), lambda i: (0, i, 0))],
                          out_specs=pl.BlockSpec((tr, LANES), lambda i: (i, 0)), out_shape=SDS((rows, LANES), F32),
                          compiler_params=_cp("parallel"), name=name)(parts)


def adamw(w, parts, m, v, row_off, name):
    rows = w.shape[0]
    n_p = parts.shape[0]
    tr = 8
    while tr * 2 <= 1024 and rows % (tr * 2) == 0 and row_off % (tr * 2) == 0:
        tr *= 2
    assert rows % tr == 0 and row_off % tr == 0
    off = row_off // tr
    c1 = 1.0 - ADAM_B1 ** ADAM_STEP
    c2 = 1.0 - ADAM_B2 ** ADAM_STEP

    def body(w_ref, p_ref, m_ref, v_ref, g_out, d_out, m_out, v_out):
        g = p_ref[0].astype(F32)
        for j in range(1, n_p):
            g = g + p_ref[j].astype(F32)
        m_new = ADAM_B1 * m_ref[...] + (1.0 - ADAM_B1) * g
        v_new = ADAM_B2 * v_ref[...] + (1.0 - ADAM_B2) * jnp.square(g)
        g_out[...] = g
        m_out[...] = m_new
        v_out[...] = v_new
        d_out[...] = -ADAM_LR * ((m_new / c1) / (jnp.sqrt(v_new / c2) + ADAM_EPS) + ADAM_WD * w_ref[...])

    spec = pl.BlockSpec((tr, LANES), lambda i: (i, 0))
    return pl.pallas_call(
        body, grid=(rows // tr,), in_specs=[spec, pl.BlockSpec((n_p, tr, LANES), lambda i: (0, off + i, 0)), spec, spec],
        out_specs=[spec] * 4, out_shape=[SDS((rows, LANES), F32)] * 4, compiler_params=_cp("parallel"), name=name)(w, parts, m, v)


FWD_NAMES = ["x", "c", "ctx", "c_ctx", "ada_w", "ada_b", "norm_g", "mlp_w1", "mlp_w2", "final_g", "ssd_w_in", "ssd_conv_w",
             "ssd_conv_b", "ssd_dt_bias", "ssd_a_log", "ssd_d", "ssd_norm_g", "ssd_w_out", "ret_w_in", "ret_log_decay",
             "ret_w_out", "hgrn_w_in", "hgrn_lb_logits", "hgrn_norm_g", "hgrn_w_out", "gdn_w_in", "gdn_conv_w", "gdn_dt_bias",
             "gdn_a_log", "gdn_norm_g", "gdn_w_out"]
WEIGHTS = FWD_NAMES[3:]
BIG = {"mlp_w1": 2, "mlp_w2": 1, "ssd_w_in": 2, "ssd_w_out": 1, "ret_w_in": 2, "ret_w_out": 1, "hgrn_w_in": 2,
       "hgrn_w_out": 1, "gdn_w_in": 2, "gdn_w_out": 1}
SMALL_SHARDED = ["norm_g", "ssd_conv_w", "gdn_conv_w", "hgrn_norm_g"]
SMALL = [n for n in WEIGHTS if n not in BIG and n != "ada_w"]


def _to_full(g8, axis):
    _, l, a, b = g8.shape
    if axis == 1:
        return g8.transpose(1, 0, 2, 3).reshape(l, N_DEV * a, b)
    return g8.transpose(1, 2, 0, 3).reshape(l, a, N_DEV * b)


def _to_shards(full, axis):
    l, a, b = full.shape
    if axis == 1:
        return full.reshape(l, N_DEV, a // N_DEV, b).transpose(1, 0, 2, 3)
    return full.reshape(l, a, N_DEV, b // N_DEV).transpose(2, 0, 1, 3)


def _flat_rows(n):
    assert n % LANES == 0
    return n // LANES


def _pad_rows(a2, rows):
    return a2 if a2.shape[-2] == rows else jnp.pad(a2, [(0, 0)] * (a2.ndim - 2) + [(0, rows - a2.shape[-2]), (0, 0)])


def _pack_small(arrs):
    parts, meta, off = [], [], 0
    for a in arrs:
        n = a.size
        npad = -(-n // LANES) * LANES
        parts.append(jnp.pad(a.reshape(-1).astype(F32), (0, npad - n)))
        meta.append((off, n, a.shape))
        off += npad
    rows = -(-(off // LANES) // 8) * 8
    flat = jnp.concatenate(parts)
    flat = jnp.pad(flat, (0, rows * LANES - off))
    return flat.reshape(rows, LANES), meta


def _unpack_small(buf, meta):
    flat = buf.reshape(buf.shape[:-2] + (-1,))
    return [flat[..., off:off + n].reshape(buf.shape[:-2] + tuple(shape)) for off, n, shape in meta]


def _my_shard(full, me, n_local):
    return lax.dynamic_slice_in_dim(full, me * n_local, n_local, axis=full.ndim - 1)


def kernel(*args):
    n_f = len(FWD_NAMES)
    inp = dict(zip(FWD_NAMES, args[:n_f]))
    tgt = args[n_f][0]
    n_w = len(WEIGHTS)
    mom_m = dict(zip(WEIGHTS, args[n_f + 1:n_f + 1 + n_w]))
    mom_v = dict(zip(WEIGHTS, args[n_f + 1 + n_w:n_f + 1 + 2 * n_w]))
    x, ctx, c = inp["x"][0], inp["ctx"][0], inp["c"]
    d = x.shape[1]
    me = 4 * lax.axis_index("x") + 2 * lax.axis_index("y") + lax.axis_index("c")

    buf, meta = _pack_small([c] + [inp[n] for n in SMALL_SHARDED])
    got = _unpack_small(exchange(buf, True, "gather_small"), meta)
    c_all = got[0].reshape(N_DEV, d)
    small_full = {}
    for n, g8 in zip(SMALL_SHARDED, got[1:]):
        small_full[n] = jnp.moveaxis(g8, 0, -2).reshape(g8.shape[1:-1] + (N_DEV * g8.shape[-1],))
    cond_in = jnp.concatenate([c_all, inp["c_ctx"][None]], axis=0)
    cond = _silu(cond_in)
    cond16 = jnp.pad(cond, ((0, 16 - cond.shape[0]), (0, 0)))

    ada_w = inp["ada_w"]
    n_ada = ada_w.shape[2]
    mod_loc = jnp.stack([_mm(cond16, ada_w[i], "nn", "ada_fwd")[:N_DEV + 1] for i in range(DEPTH)])
    mbuf, mmeta = _pack_small([mod_loc])
    (mod8,) = _unpack_small(exchange(mbuf, True, "gather_mod"), mmeta)
    mod_full = mod8.transpose(1, 2, 0, 3).reshape(DEPTH, N_DEV + 1, N_DEV * n_ada) + inp["ada_b"][:, None, :]
    mod2 = jnp.stack([mod_full[:, N_DEV], lax.dynamic_index_in_dim(mod_full, me, axis=1, keepdims=False)], axis=1)

    segs, off = {}, 0
    pieces = []
    for n, axis in BIG.items():
        w = inp[n]
        rows = _flat_rows(w.size)
        rows_pad = -(-rows // FLAT_ALIGN_ROWS) * FLAT_ALIGN_ROWS
        segs[n] = (off, rows, rows_pad)
        pieces.append(_pad_rows(w.astype(BF16).reshape(rows, LANES), rows_pad))
        off += rows_pad
    wg = exchange(jnp.concatenate(pieces, axis=0), True, "gather_weights")
    p = {}
    for n, axis in BIG.items():
        o, rows, _ = segs[n]
        p[n] = _to_full(wg[:, o:o + rows].reshape((N_DEV,) + inp[n].shape), axis)
    n_gate = p["gdn_w_in"].shape[2] - 6 * d
    p["gdn_w_in_main"], p["gdn_w_in_gate"] = p["gdn_w_in"][0, :, :6 * d], p["gdn_w_in"][0, :, 6 * d:]
    for n in ("ssd_w_in", "ssd_w_out", "ret_w_in", "ret_w_out", "hgrn_w_in", "hgrn_w_out", "gdn_w_out"):
        p[n] = p[n][0]
    del p["gdn_w_in"]
    for n in SMALL:
        if n not in ("c_ctx", "ada_b"):
            p[n] = small_full[n] if n in small_full else inp[n]
    for n in ("ssd_conv_w", "ssd_conv_b", "ssd_dt_bias", "ssd_a_log", "ssd_d", "ssd_norm_g", "ret_log_decay", "hgrn_norm_g",
              "gdn_conv_w", "gdn_dt_bias", "gdn_a_log", "gdn_norm_g"):
        p[n] = p[n][0]

    loss_loc, (g_mod2, g_x, g_p) = jax.value_and_grad(
        lambda mod2_, x_, p_: local_loss(x_, ctx, mod2_, tgt, p_), argnums=(0, 1, 2))(mod2, x, p)

    small_g_names = [n for n in SMALL if n not in ("c_ctx", "ada_b")]
    gbuf, gmeta = _pack_small([loss_loc.reshape(1), g_mod2] + [g_p[n] for n in small_g_names])
    g8 = exchange(gbuf, True, "gather_small_grads")
    gsum = _unpack_small(sum_parts(g8, "sum_small_grads"), gmeta)
    loss = gsum[0][0]
    g_small = dict(zip(small_g_names, gsum[2:]))
    for n in ("ssd_conv_w", "ssd_conv_b", "ssd_dt_bias", "ssd_a_log", "ssd_d", "ssd_norm_g", "ret_log_decay", "hgrn_norm_g",
              "gdn_conv_w", "gdn_dt_bias", "gdn_a_log", "gdn_norm_g"):
        g_small[n] = g_small[n][None]
    dmod_each = _unpack_small(g8, gmeta)[1]
    dmod9 = jnp.concatenate([dmod_each[:, :, 1].transpose(1, 0, 2), gsum[1][:, 0:1]], axis=1)
    g_small["ada_b"] = gsum[1][:, 0] + gsum[1][:, 1]
    dmod16 = jnp.pad(_my_shard(dmod9, me, n_ada), ((0, 0), (0, 16 - dmod9.shape[1]), (0, 0)))
    g_ada_w = jnp.stack([_mm(cond16, dmod16[i], "tn", "ada_dw") for i in range(DEPTH)])
    dcond_part = _mm(dmod16[0], ada_w[0], "nt", "ada_dx")
    for i in range(1, DEPTH):
        dcond_part = dcond_part + _mm(dmod16[i], ada_w[i], "nt", "ada_dx")
    cbuf, cmeta = _pack_small([dcond_part[N_DEV]])
    (dcond8,) = _unpack_small(sum_parts(exchange(cbuf, True, "gather_dcond"), "sum_dcond"), cmeta)
    g_small["c_ctx"] = jax.vjp(_silu, inp["c_ctx"])[1](dcond8)[0]

    g_p["gdn_w_in"] = jnp.concatenate([g_p.pop("gdn_w_in_main"), g_p.pop("gdn_w_in_gate")], axis=1)
    pieces = []
    for n, axis in BIG.items():
        _, rows, rows_pad = segs[n]
        gf = g_p[n] if g_p[n].ndim == 3 else g_p[n][None]
        pieces.append(_pad_rows(_to_shards(gf, axis).reshape(N_DEV, rows, LANES), rows_pad))
    parts = exchange(jnp.concatenate(pieces, axis=1), False, "scatter_grads")

    out = {}
    for n in BIG:
        o, rows, _ = segs[n]
        shp = inp[n].shape
        res = adamw(inp[n].reshape(rows, LANES), parts, mom_m[n].reshape(rows, LANES), mom_v[n].reshape(rows, LANES), o,
                    "adamw_" + n)
        out[n] = [r.reshape(shp) for r in res]
    rows = _flat_rows(ada_w.size)
    res = adamw(ada_w.reshape(rows, LANES), g_ada_w.reshape(1, rows, LANES), mom_m["ada_w"].reshape(rows, LANES),
                mom_v["ada_w"].reshape(rows, LANES), 0, "adamw_ada_w")
    out["ada_w"] = [r.reshape(ada_w.shape) for r in res]
    for n in SMALL_SHARDED:
        g_small[n] = _my_shard(g_small[n], me, inp[n].shape[-1])
    wb, wmeta = _pack_small([inp[n] for n in SMALL])
    gb, _ = _pack_small([g_small[n] for n in SMALL])
    mb, _ = _pack_small([mom_m[n] for n in SMALL])
    vb, _ = _pack_small([mom_v[n] for n in SMALL])
    res = [_unpack_small(r, wmeta) for r in adamw(wb, gb[None], mb, vb, 0, "adamw_small")]
    for j, n in enumerate(SMALL):
        out[n] = [r[j] for r in res]

    outs = [loss, g_x[None]]
    for k in range(4):
        outs += [out[n][k] for n in WEIGHTS]
    return tuple(outs)
```

```python
import functools
import math

import jax
import jax.numpy as jnp
from jax import lax
from jax.experimental import pallas as pl
from jax.experimental.pallas import tpu as pltpu

F32 = jnp.float32
BF16 = jnp.bfloat16
HI = lax.Precision.HIGHEST

N_DEV = 8
CHUNK = 64
SUB = 16
GROUPS_PER_STEP = 2
GRID_W = 64
ROPE_BASE = 10000.0
NORM_EPS = 1e-6
DEPTH = 4
SSD_GROUPS = 8
SSD_STATE = 128
SSD_HEAD_DIM = 64
RET_HEADS = 8
HEAD128 = 128
ADAM_LR, ADAM_B1, ADAM_B2, ADAM_EPS, ADAM_WD, ADAM_STEP = 0.001, 0.9, 0.999, 1e-08, 0.01, 10

V7X_VMEM_BYTES = 64 * 1024 * 1024
VMEM_LIMIT = (V7X_VMEM_BYTES * 3) // 4
ROW_PIPELINE_BYTES = V7X_VMEM_BYTES // 4
LANES = 128
MESH = pl.DeviceIdType.MESH

SDS = jax.ShapeDtypeStruct


def _cp(*sem):
    return pltpu.CompilerParams(dimension_semantics=tuple(sem), vmem_limit_bytes=VMEM_LIMIT)


def _pick(n, cap, quantum=LANES):
    best = None
    d = quantum
    while d <= min(n, cap):
        if n % d == 0:
            best = d
        d += quantum
    return n if best is None else best


def _mm(a, b, mode, name, out_dtype=F32):
    if mode == "nn":
        (m, k), n = a.shape, b.shape[1]
    elif mode == "nt":
        (m, k), n = a.shape, b.shape[0]
    else:
        (k, m), n = a.shape, b.shape[1]
    tm, tn, tk = _pick(m, 1024), _pick(n, 1280), _pick(k, 512)
    nk = k // tk
    if mode == "nn":
        a_spec = pl.BlockSpec((tm, tk), lambda i, j, kk: (i, kk))
        b_spec = pl.BlockSpec((tk, tn), lambda i, j, kk: (kk, j))
        dims = (((1,), (0,)), ((), ()))
    elif mode == "nt":
        a_spec = pl.BlockSpec((tm, tk), lambda i, j, kk: (i, kk))
        b_spec = pl.BlockSpec((tn, tk), lambda i, j, kk: (j, kk))
        dims = (((1,), (1,)), ((), ()))
    else:
        a_spec = pl.BlockSpec((tk, tm), lambda i, j, kk: (kk, i))
        b_spec = pl.BlockSpec((tk, tn), lambda i, j, kk: (kk, j))
        dims = (((0,), (0,)), ((), ()))

    def body(a_ref, b_ref, o_ref, acc_ref):
        kk = pl.program_id(2)

        @pl.when(kk == 0)
        def _():
            acc_ref[...] = jnp.zeros_like(acc_ref)

        acc_ref[...] += lax.dot_general(a_ref[...].astype(BF16), b_ref[...].astype(BF16), dims,
                                        preferred_element_type=F32)

        @pl.when(kk == nk - 1)
        def _():
            o_ref[...] = acc_ref[...].astype(o_ref.dtype)

    return pl.pallas_call(
        body, grid=(m // tm, n // tn, nk), in_specs=[a_spec, b_spec],
        out_specs=pl.BlockSpec((tm, tn), lambda i, j, kk: (i, j)),
        out_shape=SDS((m, n), out_dtype), scratch_shapes=[pltpu.VMEM((tm, tn), F32)],
        compiler_params=_cp("parallel", "parallel", "arbitrary"), name=name)(a, b)


def pmatmul(a, w, name):
    @jax.custom_vjp
    def op(a, w):
        return _mm(a, w, "nn", name + "_fwd")

    def fwd(a, w):
        return _mm(a, w, "nn", name + "_fwd"), (a, w)

    def bwd(res, g):
        a, w = res
        return _mm(g, w, "nt", name + "_dx"), _mm(a, g, "tn", name + "_dw", out_dtype=w.dtype)

    op.defvjp(fwd, bwd)
    return op(a, w)


def rowop(f, name, rows, pars, out_ws, tile, ctx_rows=0):
    t = rows[0].shape[0]
    tile = math.gcd(math.gcd(tile, t), ctx_rows if ctx_rows else t)
    row_bytes = 2 * 4 * (2 * sum(r.shape[1] for r in rows) + sum(out_ws))
    while tile > 8 and tile % 2 == 0 and tile * row_bytes > ROW_PIPELINE_BYTES:
        tile //= 2
    nt = t // tile
    cut = ctx_rows // tile
    n_r, n_p, n_o = len(rows), len(pars), len(out_ws)
    segs = [p.shape[0] for p in pars]
    assert all(s in (1, 2) for s in segs)

    def par_map(s):
        if s == 1:
            return lambda i: (0, 0, 0)
        return lambda i: (jnp.where(i >= cut, 1, 0), 0, 0)

    row_specs = [pl.BlockSpec((tile, r.shape[1]), lambda i: (i, 0)) for r in rows]
    par_specs = [pl.BlockSpec((None, 1, p.shape[2]), par_map(p.shape[0])) for p in pars]
    out_specs = [pl.BlockSpec((tile, w), lambda i: (i, 0)) for w in out_ws]
    out_shape = [SDS((t, w), F32) for w in out_ws]

    def fwd_call(*args):
        def body(*refs):
            outs = f(*[r[...] for r in refs[:n_r + n_p]])
            for o_ref, o in zip(refs[n_r + n_p:], outs):
                o_ref[...] = o

        return tuple(pl.pallas_call(body, grid=(nt,), in_specs=row_specs + par_specs, out_specs=out_specs,
                                    out_shape=out_shape, compiler_params=_cp("parallel"), name=name + "_fwd")(*args))

    def bwd_call(args, gouts):
        def body(*refs):
            ins = [r[...] for r in refs[:n_r + n_p]]
            gs = tuple(r[...] for r in refs[n_r + n_p:n_r + n_p + n_o])
            d_refs = refs[n_r + n_p + n_o:]
            _, vjp = jax.vjp(f, *ins)
            grads = vjp(gs)
            for ref, g in zip(d_refs[:n_r], grads[:n_r]):
                ref[...] = g
            i = pl.program_id(0)
            for ref, g, s in zip(d_refs[n_r:], grads[n_r:], segs):
                first = (i == 0) if s == 1 else jnp.logical_or(i == 0, i == cut)

                @pl.when(first)
                def _(ref=ref, g=g):
                    ref[...] = g

                @pl.when(jnp.logical_not(first))
                def _(ref=ref, g=g):
                    ref[...] += g

        d_specs = row_specs + par_specs
        d_shape = [SDS(r.shape, F32) for r in rows] + [SDS(p.shape, F32) for p in pars]
        return tuple(pl.pallas_call(body, grid=(nt,), in_specs=row_specs + par_specs + out_specs, out_specs=d_specs,
                                    out_shape=d_shape, compiler_params=_cp("arbitrary"), name=name + "_bwd")(*args, *gouts))

    @jax.custom_vjp
    def op(*args):
        return fwd_call(*args)

    op.defvjp(lambda *args: (fwd_call(*args), args), bwd_call)
    return op(*rows, *pars)


def scanop(chunk_fn, name, cols, col_ws, grps, state_shape, y_w, rev=False, ctx_rows=0, unroll=GROUPS_PER_STEP):
    t = cols[0].shape[0]
    g_n = cols[0].shape[1] // col_ws[0]
    u_n = unroll if g_n % unroll == 0 else 1
    gs_n = g_n // u_n
    nc = t // CHUNK
    ncx = ctx_rows // CHUNK
    n_c, n_g = len(cols), len(grps)

    def order(c):
        if not rev:
            return c
        return jnp.where(c < ncx, ncx - 1 - c, nc - 1 - (c - ncx))

    def specs(cmap):
        col_specs = [pl.BlockSpec((CHUNK, u_n * w), lambda g, c: (cmap(c), g)) for w in col_ws]
        grp_specs = [pl.BlockSpec((u_n, CHUNK, a.shape[2]), lambda g, c: (g, cmap(c), 0)) for a in grps]
        y_spec = pl.BlockSpec((CHUNK, u_n * y_w), lambda g, c: (cmap(c), g))
        s_spec = pl.BlockSpec((u_n, None) + state_shape, lambda g, c: (g, cmap(c), 0, 0))
        return col_specs, grp_specs, y_spec, s_spec

    def group_ins(ins, u):
        return ([a[:, u * w:(u + 1) * w] for a, w in zip(ins[:n_c], col_ws)] + [a[u] for a in ins[n_c:]])

    def fwd_call(*args):
        col_specs, grp_specs, y_spec, s_spec = specs(order)

        def body(*refs):
            ins = [r[...] for r in refs[:n_c + n_g]]
            y_ref, sall_ref, s_ref = refs[n_c + n_g:]

            @pl.when(pl.program_id(1) == 0)
            def _():
                s_ref[...] = jnp.zeros_like(s_ref)

            s = s_ref[...]
            sall_ref[...] = s
            res = [chunk_fn(s[u], *group_ins(ins, u)) for u in range(u_n)]
            y_ref[...] = jnp.concatenate([y for _, y in res], axis=1)
            s_ref[...] = jnp.stack([s_new for s_new, _ in res])

        return pl.pallas_call(
            body, grid=(gs_n, nc), in_specs=col_specs + grp_specs, out_specs=[y_spec, s_spec],
            out_shape=[SDS((t, g_n * y_w), F32), SDS((g_n, nc) + state_shape, F32)],
            scratch_shapes=[pltpu.VMEM((u_n,) + state_shape, F32)],
            compiler_params=_cp("parallel", "arbitrary"), name=name + "_fwd")(*args)

    def bwd_call(res, gy):
        args, s_all = res
        col_specs, grp_specs, y_spec, s_spec = specs(lambda c: order(nc - 1 - c))

        def body(*refs):
            ins = [r[...] for r in refs[:n_c + n_g]]
            s_prev = refs[n_c + n_g][...]
            dy = refs[n_c + n_g + 1][...]
            d_refs = refs[n_c + n_g + 2:-1]
            ds_ref = refs[-1]

            @pl.when(pl.program_id(1) == 0)
            def _():
                ds_ref[...] = jnp.zeros_like(ds_ref)

            ds = ds_ref[...]
            grads = []
            for u in range(u_n):
                _, vjp = jax.vjp(chunk_fn, s_prev[u], *group_ins(ins, u))
                grads.append(vjp((ds[u], dy[:, u * y_w:(u + 1) * y_w])))
            ds_ref[...] = jnp.stack([g[0] for g in grads])
            for j, ref in enumerate(d_refs):
                ref[...] = (jnp.concatenate([g[1 + j] for g in grads], axis=1) if j < n_c
                            else jnp.stack([g[1 + j] for g in grads]))

        return tuple(pl.pallas_call(
            body, grid=(gs_n, nc), in_specs=col_specs + grp_specs + [s_spec, y_spec], out_specs=col_specs + grp_specs,
            out_shape=[SDS(a.shape, F32) for a in args],
            scratch_shapes=[pltpu.VMEM((u_n,) + state_shape, F32)],
            compiler_params=_cp("parallel", "arbitrary"), name=name + "_bwd")(*args, s_all, gy))

    @jax.custom_vjp
    def op(*args):
        return fwd_call(*args)[0]

    def fwd(*args):
        y, s_all = fwd_call(*args)
        return y, (args, s_all)

    op.defvjp(fwd, bwd_call)
    return op(*cols, *grps)


def _dg(a, b, ca, cb, prec=None):
    return lax.dot_general(a, b, (((ca,), (cb,)), ((), ())), precision=prec, preferred_element_type=F32)


@functools.partial(jax.custom_vjp, nondiff_argnums=(2, 3))
def mmb(a, b, ca, cb):
    return _dg(a.astype(BF16), b.astype(BF16), ca, cb)


def _mmb_fwd(a, b, ca, cb):
    return mmb(a, b, ca, cb), (a, b)


def _mmb_bwd(ca, cb, res, g):
    a, b = res
    ab, bb, gb = a.astype(BF16), b.astype(BF16), g.astype(BF16)
    da = _dg(gb, bb, 1, 1 - cb) if ca == 1 else _dg(bb, gb, 1 - cb, 1)
    db = _dg(ab, gb, 1 - ca, 0) if cb == 0 else _dg(gb, ab, 0, 1 - ca)
    return da, db


mmb.defvjp(_mmb_fwd, _mmb_bwd)


def mmh(a, b, ca=1, cb=0):
    return _dg(a, b, ca, cb, HI)


def _masks(n, rev=False):
    i = lax.broadcasted_iota(jnp.int32, (n, n), 0)
    j = lax.broadcasted_iota(jnp.int32, (n, n), 1)
    return (i <= j, i < j) if rev else (i >= j, i > j)


def _silu(x):
    return x * jax.nn.sigmoid(x)


def _softplus(x):
    return jnp.maximum(x, 0.0) + jnp.log(1.0 + jnp.exp(-jnp.abs(x)))


def _expand_heads(p, width):
    h = p.shape[1]
    lane = lax.broadcasted_iota(jnp.int32, (h, h * width), 1)
    row = lax.broadcasted_iota(jnp.int32, (h, h * width), 0)
    e = jnp.where(lane // width == row, 1.0, 0.0).astype(F32)
    return mmh(jnp.broadcast_to(p, (8, h)), e)[0:1]


def _group_apply(x, width, fn):
    n = x.shape[1] // width
    return jnp.concatenate([fn(x[:, g * width:(g + 1) * width]) for g in range(n)], axis=1)


def _rms(x):
    return x * lax.rsqrt(jnp.mean(x * x, axis=-1, keepdims=True) + NORM_EPS)


@jax.custom_vjp
def unit_lower_inv(l):
    n = l.shape[0]
    eye = jnp.where(_masks(n)[0] & jnp.logical_not(_masks(n)[1]), 1.0, 0.0).astype(F32)
    p = -l
    x = eye + p
    for _ in range(int(math.log2(n)) - 1):
        p = mmh(p, p)
        x = x + mmh(x, p)
    return x


def _uli_fwd(l):
    x = unit_lower_inv(l)
    return x, x


def _uli_bwd(x, g):
    return (-_dg(_dg(x, g, 0, 0, HI), x, 1, 1, HI),)


unit_lower_inv.defvjp(_uli_fwd, _uli_bwd)


def scalar_decay_chunk(r_n, n, p, with_dt, rev):
    def fn(s, q, k, v, la, *rest):
        c = q.shape[0]
        incl, _ = _masks(c, rev)
        tril = incl.astype(F32)
        cum = mmh(tril, la)
        cum_t = _dg(la, tril, 0, 1, HI)
        tot = cum[0:1, :] if rev else cum[c - 1:c, :]
        scores = mmb(q, k, 1, 1)
        ys, ss = [], []
        for r in range(r_n):
            cr = cum[:, r:r + 1]
            dec = jnp.exp(jnp.where(incl, cr - cum_t[r:r + 1, :], -jnp.inf))
            vr = v[:, r * p:(r + 1) * p]
            if with_dt:
                vr = vr * rest[0][:, r:r + 1]
            sr = s[r * n:(r + 1) * n, :]
            ys.append(mmb(scores * dec, vr, 1, 0) + mmb(q, sr, 1, 0) * jnp.exp(cr))
            ss.append(jnp.exp(tot[:, r:r + 1]) * sr + mmb(k, jnp.exp(tot[:, r:r + 1] - cr) * vr, 0, 0))
        return jnp.concatenate(ss, axis=0), jnp.concatenate(ys, axis=1)

    return fn


def vector_decay_chunk(rev):
    return functools.partial(_vector_decay_chunk, rev)


def _vector_decay_chunk(rev, s, q, k, v, lf):
    c, kd = q.shape
    incl, _ = _masks(c, rev)
    cum = mmh(incl.astype(F32), lf)
    tot = cum[0:1, :] if rev else cum[c - 1:c, :]
    tot_col = _dg(lf, jnp.ones((c, 1), F32), 0, 0, HI)
    y_inter = mmb(q * jnp.exp(cum), s, 1, 0)
    s_new = jnp.exp(tot_col) * s + mmb(k * jnp.exp(tot - cum), v, 0, 0)
    i3 = lax.broadcasted_iota(jnp.int32, (SUB, SUB, 1), 0)
    j3 = lax.broadcasted_iota(jnp.int32, (SUB, SUB, 1), 1)
    ys = []
    for b in range(c // SUB):
        lo, hi = b * SUB, (b + 1) * SUB
        qb, kb, cb, vb = q[lo:hi], k[lo:hi], cum[lo:hi], v[lo:hi]
        dec = jnp.exp(jnp.where((i3 <= j3) if rev else (i3 >= j3), cb[:, None, :] - cb[None, :, :], -jnp.inf))
        a_diag = jnp.sum(qb[:, None, :] * kb[None, :, :] * dec, axis=-1)
        yb = mmb(a_diag, vb, 1, 0)
        if rev and hi < c:
            base = cum[hi:hi + 1, :]
            a_off = mmb(qb * jnp.exp(cb - base), k[hi:] * jnp.exp(base - cum[hi:]), 1, 1)
            yb = yb + mmb(a_off, v[hi:], 1, 0)
        if not rev and b > 0:
            base = cum[lo - 1:lo, :]
            a_off = mmb(qb * jnp.exp(cb - base), k[:lo] * jnp.exp(base - cum[:lo]), 1, 1)
            yb = yb + mmb(a_off, v[:lo], 1, 0)
        ys.append(yb)
    return s_new, y_inter + jnp.concatenate(ys, axis=0)


def delta_chunk(r_n, kd, vd, rev):
    def fn(s, q, k, v, beta, la):
        c = q.shape[0]
        incl, strict = _masks(c, rev)
        tril = incl.astype(F32)
        cum = mmh(tril, la)
        cum_t = _dg(la, tril, 0, 1, HI)
        tot = cum[0:1, :] if rev else cum[c - 1:c, :]
        kk = mmb(k, k, 1, 1)
        qk = mmb(q, k, 1, 1)
        ys, ss = [], []
        for r in range(r_n):
            cr, br = cum[:, r:r + 1], beta[:, r:r + 1]
            seg = cr - cum_t[r:r + 1, :]
            x = unit_lower_inv(br * kk * jnp.exp(jnp.where(strict, seg, -jnp.inf)))
            vr = v[:, r * vd:(r + 1) * vd]
            sr = s[r * kd:(r + 1) * kd, :]
            u = mmh(x, vr * br)
            w = mmh(x, k * (br * jnp.exp(cr)))
            v_new = u - mmb(w, sr, 1, 0)
            attn = qk * jnp.exp(jnp.where(incl, seg, -jnp.inf))
            ys.append(mmb(attn, v_new, 1, 0) + mmb(q, sr, 1, 0) * jnp.exp(cr))
            ss.append(jnp.exp(tot[:, r:r + 1]) * sr + mmb(k, jnp.exp(tot[:, r:r + 1] - cr) * v_new, 0, 0))
        return jnp.concatenate(ss, axis=0), jnp.concatenate(ys, axis=1)

    return fn


def f_adaln(x, g, sh, sc):
    return ((_rms(x) * g) * (1.0 + sc) + sh,)


def f_resid(x, y, gate):
    return (x + gate * y,)


def f_relu2(h):
    return (jnp.square(jnp.maximum(h, 0.0)),)


def f_conv_bias(up, u, un, w0, w1, w2, b):
    return (_silu(w0 * up + w1 * u + w2 * un + b),)


def f_conv(up, u, un, w0, w1, w2):
    return (_silu(w0 * up + w1 * u + w2 * un),)


def f_dt(raw, bias, a_log):
    dt = _softplus(raw + bias)
    return dt, -jnp.exp(a_log) * dt


def f_ssd_post(yf, yb, xs, z, d_skip, ng):
    y = (yf + yb + _expand_heads(d_skip, SSD_HEAD_DIM) * xs) * _silu(z)
    return (_group_apply(y, y.shape[1] // SSD_GROUPS, _rms) * ng,)


def f_rope(scale):
    def f(q, qsw, cos, sin):
        return ((q * cos + qsw * sin) * scale,)
    return f


def f_ret_post(yf, yb, g):
    def ln(y):
        mu = jnp.mean(y, axis=-1, keepdims=True)
        var = jnp.mean(jnp.square(y - mu), axis=-1, keepdims=True)
        return (y - mu) * lax.rsqrt(var + NORM_EPS)
    y = yf + yb
    return (_group_apply(y, y.shape[1] // RET_HEADS, ln) * _silu(g),)


def f_lower_bound(layer):
    def f(logits):
        m = jnp.max(logits, axis=0, keepdims=True)
        e = jnp.exp(logits - m)
        p = e / jnp.sum(e, axis=0, keepdims=True)
        lb = p[1:2]
        for j in range(2, layer + 1):
            lb = lb + p[j:j + 1]
        return (jnp.broadcast_to(lb, logits.shape),)
    return f


def f_hgrn_gates(ff, fb, lb):
    def gates(f):
        log_sig = jnp.minimum(f, 0.0) - jnp.log(1.0 + jnp.exp(-jnp.abs(f)))
        a, b = jnp.log(lb), jnp.log(1.0 - lb) + log_sig
        m = jnp.maximum(a, b)
        return m + jnp.log(jnp.exp(a - m) + jnp.exp(b - m)), (1.0 - lb) * jax.nn.sigmoid(-f)
    lf_f, k_f = gates(ff)
    lf_b, k_b = gates(fb)
    return lf_f, k_f, lf_b, k_b


def f_hgrn_post(yf, yb, g, ng):
    return (_group_apply(yf + yb, HEAD128, _rms) * ng * _silu(g),)


def f_gdn_qk(q, k):
    def l2(x):
        return x * lax.rsqrt(jnp.sum(x * x, axis=-1, keepdims=True) + 1e-6)
    return _group_apply(q, HEAD128, l2) * HEAD128 ** -0.5, _group_apply(k, HEAD128, l2)


def f_gdn_gates(bt, a, dt_bias, a_log):
    return jax.nn.sigmoid(bt), -jnp.exp(a_log) * _softplus(a + dt_bias)


def f_gdn_post(yf, yb, z, ng):
    n = yf.shape[1] // HEAD128
    return (_group_apply(yf + yb, HEAD128, _rms) * jnp.concatenate([ng] * n, axis=1) * _silu(z),)


def f_loss(x, tgt, fg):
    err = jnp.square(_rms(x) * fg - tgt)
    return (jnp.broadcast_to(jnp.mean(err, axis=-1, keepdims=True), (x.shape[0], LANES)),)


def _par(v):
    return v.reshape(1, 1, -1)


def _shift(u, lc):
    z = jnp.zeros((1, u.shape[1]), u.dtype)
    prev = jnp.concatenate([z, u[:lc - 1], z, u[lc:-1]], axis=0)
    nxt = jnp.concatenate([u[1:lc], z, u[lc + 1:], z], axis=0)
    return prev, nxt


def _grp(a, g_n):
    t = a.shape[0]
    return a.reshape(t, g_n, -1).transpose(1, 0, 2)


def ssd_mixer(h, p, lc, start):
    d = h.shape[1]
    di = 2 * d
    gn = SSD_GROUPS * SSD_STATE
    heads = di // SSD_HEAD_DIM
    r_n = heads // SSD_GROUPS
    u = pmatmul(h, p["ssd_w_in"], "ssd_in")
    z, xbc, dtr = u[:, :di], u[:, di:di + di + 2 * gn], u[:, 2 * di + 2 * gn:]
    up, un = _shift(xbc, lc)
    cw = p["ssd_conv_w"]
    (xbc,) = rowop(f_conv_bias, "ssd_conv", [up, xbc, un], [_par(cw[0]), _par(cw[1]), _par(cw[2]), _par(p["ssd_conv_b"])],
                   [xbc.shape[1]], 128)
    xs, bm, cm = xbc[:, :di], xbc[:, di:di + gn], xbc[:, di + gn:]
    dt, la = rowop(f_dt, "ssd_dt", [dtr], [_par(p["ssd_dt_bias"]), _par(p["ssd_a_log"])], [2 * heads, 2 * heads], 256)

    def scan(tag, rev, grps):
        return scanop(scalar_decay_chunk(r_n, SSD_STATE, SSD_HEAD_DIM, True, rev), "ssd_scan_" + tag, [cm, bm, xs],
                      [SSD_STATE, SSD_STATE, r_n * SSD_HEAD_DIM], grps, (r_n * SSD_STATE, SSD_HEAD_DIM),
                      r_n * SSD_HEAD_DIM, rev, lc)

    yf = scan("f", False, [_grp(la[:, :heads], SSD_GROUPS), _grp(dt[:, :heads], SSD_GROUPS)])
    yb = scan("b", True, [_grp(la[:, heads:], SSD_GROUPS), _grp(dt[:, heads:], SSD_GROUPS)])
    (yn,) = rowop(f_ssd_post, "ssd_post", [yf[start:], yb[start:], xs[start:], z[start:]],
                  [_par(p["ssd_d"]), _par(p["ssd_norm_g"])], [di], 128)
    return pmatmul(yn, p["ssd_w_out"], "ssd_out")


def _rope_tables(n_lat, lc, d, heads):
    qk = d // heads
    half = qk // 2
    quarter = half // 2
    pos = jnp.arange(n_lat)
    inv_freq = ROPE_BASE ** (-jnp.arange(0, half, 2, dtype=F32) / half)

    def tab(p):
        ang = p.astype(F32)[:, None] * inv_freq
        return jnp.cos(ang), jnp.sin(ang)

    cr, sr = tab(pos // GRID_W)
    cc, sc = tab(pos % GRID_W)
    cos = jnp.concatenate([cr, cr, cc, cc], axis=1)
    sin = jnp.concatenate([-sr, sr, -sc, sc], axis=1)
    cos = jnp.concatenate([jnp.ones((lc, qk), F32), cos], axis=0)
    sin = jnp.concatenate([jnp.zeros((lc, qk), F32), sin], axis=0)
    return jnp.tile(cos, (1, heads)), jnp.tile(sin, (1, heads)), quarter


def retention_mixer(h, p, lc, start):
    t, d = h.shape
    dv = 2 * d
    qk = d // RET_HEADS
    u = pmatmul(h, p["ret_w_in"], "ret_in")
    q, k, v, g = u[:, :d], u[:, d:2 * d], u[:, 2 * d:2 * d + dv], u[:, 2 * d + dv:]
    cos, sin, quarter = _rope_tables(t - lc, lc, d, RET_HEADS)
    cos, sin = lax.stop_gradient(cos), lax.stop_gradient(sin)

    def swap(a):
        return a.reshape(t, -1, 2, quarter)[:, :, ::-1, :].reshape(t, d)

    (q,) = rowop(f_rope(1.0), "ret_rope_q", [q, swap(q), cos, sin], [], [d], 256)
    (k,) = rowop(f_rope(qk ** -0.5), "ret_rope_k", [k, swap(k), cos, sin], [], [d], 256)

    def scan(tag, rev, grps):
        return scanop(scalar_decay_chunk(1, qk, dv // RET_HEADS, False, rev), "ret_scan_" + tag, [q, k, v],
                      [qk, qk, dv // RET_HEADS], grps, (qk, dv // RET_HEADS), dv // RET_HEADS, rev, lc)

    ld = p["ret_log_decay"]
    la_f = jnp.broadcast_to(ld[0][:, None, None], (RET_HEADS, t, 1))
    la_b = jnp.broadcast_to(ld[1][:, None, None], (RET_HEADS, t, 1))
    yf, yb = scan("f", False, [la_f]), scan("b", True, [la_b])
    (yn,) = rowop(f_ret_post, "ret_post", [yf[start:], yb[start:], g[start:]], [], [dv], 128)
    return pmatmul(yn, p["ret_w_out"], "ret_out")


def hgrn2_mixer(h, p, lc, start, layer):
    t, d = h.shape
    u = pmatmul(h, p["hgrn_w_in"], "hgrn_in")
    q, f_f, f_b, i, g = (u[:, j * d:(j + 1) * d] for j in range(5))
    (lb,) = rowop(f_lower_bound(layer), "hgrn_lb", [p["hgrn_lb_logits"]], [], [d], DEPTH)
    lf_f, k_f, lf_b, k_b = rowop(f_hgrn_gates, "hgrn_gates", [f_f, f_b], [lb[0:1][None]], [d] * 4, 256)

    def scan(tag, rev, cols):
        return scanop(vector_decay_chunk(rev), "hgrn_scan_" + tag, cols, [HEAD128] * 4, [], (HEAD128, HEAD128), HEAD128,
                      rev, lc)

    yf = scan("f", False, [q, k_f, i, lf_f])
    yb = scan("b", True, [q, k_b, i, lf_b])
    (yn,) = rowop(f_hgrn_post, "hgrn_post", [yf[start:], yb[start:], g[start:]], [_par(p["hgrn_norm_g"])], [d], 256)
    return pmatmul(yn, p["hgrn_w_out"], "hgrn_out")


def gdn_mixer(h, p, lc, start):
    t, d = h.shape
    dk, dv = d, 2 * d
    kh = d // HEAD128
    r_n = 2
    cc = 2 * dk + dv
    u = pmatmul(h, p["gdn_w_in_main"], "gdn_in")
    ug = pmatmul(h, p["gdn_w_in_gate"], "gdn_in_gate")
    qkv, z = u[:, :cc], u[:, cc:]
    nb = 2 * kh * r_n
    bt, a = ug[:, :nb], ug[:, nb:]
    up, un = _shift(qkv, lc)
    cw = p["gdn_conv_w"]
    (qkv,) = rowop(f_conv, "gdn_conv", [up, qkv, un], [_par(cw[0]), _par(cw[1]), _par(cw[2])], [cc], 128)
    q, k, v = qkv[:, :dk], qkv[:, dk:2 * dk], qkv[:, 2 * dk:]
    q, k = rowop(f_gdn_qk, "gdn_qk", [q, k], [], [dk, dk], 256)
    beta, la = rowop(f_gdn_gates, "gdn_gates", [bt, a], [_par(p["gdn_dt_bias"]), _par(p["gdn_a_log"])], [nb, nb], 256)

    def scan(tag, rev, grps):
        return scanop(delta_chunk(r_n, HEAD128, HEAD128, rev), "gdn_scan_" + tag, [q, k, v],
                      [HEAD128, HEAD128, r_n * HEAD128], grps, (r_n * HEAD128, HEAD128), r_n * HEAD128, rev, lc)

    half = nb // 2
    yf = scan("f", False, [_grp(beta[:, :half], kh), _grp(la[:, :half], kh)])
    yb = scan("b", True, [_grp(beta[:, half:], kh), _grp(la[:, half:], kh)])
    (yn,) = rowop(f_gdn_post, "gdn_post", [yf[start:], yb[start:], z[start:]], [_par(p["gdn_norm_g"])], [dv], 128)
    return pmatmul(yn, p["gdn_w_out"], "gdn_out")


def local_loss(x, ctx, mod2, tgt, p):
    lc, d = ctx.shape
    xc = jnp.concatenate([ctx, x], axis=0)
    for i in range(DEPTH):
        keep = i < DEPTH - 1
        start = 0 if keep else lc
        sh1, sc1, g1, sh2, sc2, g2 = (mod2[i][:, j * d:(j + 1) * d][:, None, :] for j in range(6))
        (h,) = rowop(f_adaln, "adaln_a%d" % i, [xc], [_par(p["norm_g"][i, 0]), sh1, sc1], [d], 256, lc)
        if i % 4 == 0:
            y = ssd_mixer(h, p, lc, start)
        elif i % 4 == 1:
            y = retention_mixer(h, p, lc, start)
        elif i % 4 == 2:
            y = hgrn2_mixer(h, p, lc, start, i)
        else:
            y = gdn_mixer(h, p, lc, start)
        if not keep:
            xc, g1, sh2, sc2, g2, lc = xc[lc:], g1[1:], sh2[1:], sc2[1:], g2[1:], 0
        (xc,) = rowop(f_resid, "resid_a%d" % i, [xc, y], [g1], [d], 256, lc)
        (h2,) = rowop(f_adaln, "adaln_b%d" % i, [xc], [_par(p["norm_g"][i, 1]), sh2, sc2], [d], 256, lc)
        a = pmatmul(h2, p["mlp_w1"][i], "mlp1_%d" % i)
        (a,) = rowop(f_relu2, "relu2_%d" % i, [a], [], [a.shape[1]], 128)
        m = pmatmul(a, p["mlp_w2"][i], "mlp2_%d" % i)
        (xc,) = rowop(f_resid, "resid_b%d" % i, [xc, m], [g2], [d], 256, lc)
    (rows,) = rowop(f_loss, "loss", [xc, tgt], [_par(p["final_g"])], [LANES], 256)
    return 0.5 * jnp.sum(rows[:, 0])


def exchange(x, name):
    def body(x_ref, o_ref, send_sems, recv_sems, local_sem):
        mx, my, mc = lax.axis_index("x"), lax.axis_index("y"), lax.axis_index("c")
        me = 4 * mx + 2 * my + mc
        copies = []
        for k in range(1, N_DEV):
            px = 1 - mx if (k >> 2) & 1 else mx
            py = 1 - my if (k >> 1) & 1 else my
            pc = 1 - mc if k & 1 else mc
            cp = pltpu.make_async_remote_copy(src_ref=x_ref, dst_ref=o_ref.at[me], send_sem=send_sems.at[k - 1],
                                              recv_sem=recv_sems.at[k - 1], device_id=(px, py, pc), device_id_type=MESH)
            cp.start()
            copies.append(cp)
        mine = pltpu.make_async_copy(x_ref, o_ref.at[me], local_sem)
        mine.start()
        for cp in copies:
            cp.wait_recv()
        for cp in copies:
            cp.wait_send()
        mine.wait()

    return pl.pallas_call(
        body, out_shape=SDS((N_DEV,) + x.shape, x.dtype),
        in_specs=[pl.BlockSpec(memory_space=pl.ANY)], out_specs=pl.BlockSpec(memory_space=pl.ANY),
        scratch_shapes=[pltpu.SemaphoreType.DMA((N_DEV - 1,)), pltpu.SemaphoreType.DMA((N_DEV - 1,)),
                        pltpu.SemaphoreType.DMA],
        name=name)(x)


def _place():
    mx, my, mc = lax.axis_index("x"), lax.axis_index("y"), lax.axis_index("c")
    chips = [(1 - mx, my), (mx, 1 - my), (1 - mx, 1 - my)]
    return mx, my, mc, chips


def _any_specs(n):
    return [pl.BlockSpec(memory_space=pl.ANY)] * n


def gather_two_level(arrs, name):
    n = len(arrs)

    def body(*refs):
        x_refs, o_refs = refs[:n], refs[n:2 * n]
        send_sems, recv_sems, local_sems = refs[2 * n:]
        mx, my, mc, chips = _place()
        me, sib = (mx, my, mc), (mx, my, 1 - mc)

        def slot(px, py, pc):
            return 4 * px + 2 * py + pc

        def copy(a, k, block, to, src=None):
            dst = o_refs[a].at[slot(*block)]
            return pltpu.make_async_remote_copy(src_ref=dst if src is None else src, dst_ref=dst, send_sem=send_sems.at[a, k],
                                                recv_sem=recv_sems.at[a, k], device_id=to, device_id_type=MESH)

        mine = [pltpu.make_async_copy(x_refs[a], o_refs[a].at[slot(*me)], local_sems.at[a]) for a in range(n)]
        for cp in mine:
            cp.start()
        sent = []
        for a in range(n):
            sent += [copy(a, 1 + j, me, (*chip, mc), src=x_refs[a]) for j, chip in enumerate(chips)]
        for a in range(n):
            sent.append(copy(a, 0, me, sib, src=x_refs[a]))
        for cp in sent:
            cp.start()
        for a in range(n):
            for j, chip in enumerate(chips):
                copy(a, 1 + j, (*chip, mc), me).wait_recv()
                fwd = copy(a, 4 + j, (*chip, mc), sib)
                fwd.start()
                sent.append(fwd)
        for a in range(n):
            copy(a, 0, sib, me).wait_recv()
            for j, chip in enumerate(chips):
                copy(a, 4 + j, (*chip, 1 - mc), me).wait_recv()
        for cp in sent:
            cp.wait_send()
        for cp in mine:
            cp.wait()

    return pl.pallas_call(
        body, out_shape=[SDS((N_DEV,) + a.shape, a.dtype) for a in arrs], in_specs=_any_specs(n), out_specs=_any_specs(n),
        scratch_shapes=[pltpu.SemaphoreType.DMA((n, 7)), pltpu.SemaphoreType.DMA((n, 7)), pltpu.SemaphoreType.DMA((n,))],
        name=name)(*arrs)


def scatter_sibling(parts, name):
    n = len(parts)

    def body(*refs):
        x_refs, o_refs = refs[:n], refs[n:2 * n]
        send_sems, recv_sems = refs[2 * n:]
        mx, my, mc, _ = _place()
        copies = []
        for a in range(n):
            for b in range(4):
                cp = pltpu.make_async_remote_copy(src_ref=x_refs[a].at[b, 1 - mc], dst_ref=o_refs[a].at[b],
                                                  send_sem=send_sems.at[a, b], recv_sem=recv_sems.at[a, b],
                                                  device_id=(mx, my, 1 - mc), device_id_type=MESH)
                cp.start()
                copies.append(cp)
        for cp in copies:
            cp.wait_recv()
        for cp in copies:
            cp.wait_send()

    return pl.pallas_call(
        body, out_shape=[SDS((4,) + p.shape[2:], p.dtype) for p in parts], in_specs=_any_specs(n), out_specs=_any_specs(n),
        scratch_shapes=[pltpu.SemaphoreType.DMA((n, 4)), pltpu.SemaphoreType.DMA((n, 4))], name=name)(*parts)


def pair_sum(mine, theirs, name):
    _, rows, cols = mine.shape
    tr = _pick(rows, max(8, (1 << 19) // cols), 8)

    def body(a_ref, b_ref, o_ref):
        o_ref[...] = (a_ref[...].astype(F32) + b_ref[...].astype(F32)).astype(o_ref.dtype)

    spec = pl.BlockSpec((None, tr, cols), lambda b, i: (b, i, 0))
    return pl.pallas_call(body, grid=(4, rows // tr), in_specs=[spec, spec], out_specs=spec,
                          out_shape=SDS(mine.shape, mine.dtype), compiler_params=_cp("parallel", "parallel"), name=name)(mine, theirs)


def scatter_chips(sums, name):
    n = len(sums)

    def body(*refs):
        x_refs, o_refs = refs[:n], refs[n:2 * n]
        send_sems, recv_sems, local_sems = refs[2 * n:]
        mx, my, mc, chips = _place()
        my_chip = 2 * mx + my
        mine = [pltpu.make_async_copy(x_refs[a].at[my_chip], o_refs[a].at[my_chip], local_sems.at[a]) for a in range(n)]
        for cp in mine:
            cp.start()
        copies = []
        for a in range(n):
            for j, (px, py) in enumerate(chips):
                cp = pltpu.make_async_remote_copy(src_ref=x_refs[a].at[2 * px + py], dst_ref=o_refs[a].at[my_chip],
                                                  send_sem=send_sems.at[a, j], recv_sem=recv_sems.at[a, j],
                                                  device_id=(px, py, mc), device_id_type=MESH)
                cp.start()
                copies.append(cp)
        for cp in copies:
            cp.wait_recv()
        for cp in copies:
            cp.wait_send()
        for cp in mine:
            cp.wait()

    return pl.pallas_call(
        body, out_shape=[SDS(p.shape, p.dtype) for p in sums], in_specs=_any_specs(n), out_specs=_any_specs(n),
        scratch_shapes=[pltpu.SemaphoreType.DMA((n, 3)), pltpu.SemaphoreType.DMA((n, 3)), pltpu.SemaphoreType.DMA((n,))],
        name=name)(*sums)


def sum_parts(parts, name):
    n_p, rows, _ = parts.shape
    tr = _pick(rows, 1024, 8)

    def body(p_ref, o_ref):
        acc = p_ref[0].astype(F32)
        for j in range(1, n_p):
            acc = acc + p_ref[j].astype(F32)
        o_ref[...] = acc

    return pl.pallas_call(body, grid=(rows // tr,), in_specs=[pl.BlockSpec((n_p, tr, LANES), lambda i: (0, i, 0))],
                          out_specs=pl.BlockSpec((tr, LANES), lambda i: (i, 0)), out_shape=SDS((rows, LANES), F32),
                          compiler_params=_cp("parallel"), name=name)(parts)


def adamw(w, parts, m, v, name):
    rows, cols = w.shape
    n_p = parts.shape[0]
    tr = _pick(rows, max(8, (1 << 18) // cols), 8)
    c1 = 1.0 - ADAM_B1 ** ADAM_STEP
    c2 = 1.0 - ADAM_B2 ** ADAM_STEP

    def body(w_ref, p_ref, m_ref, v_ref, g_out, d_out, m_out, v_out):
        g = p_ref[0].astype(F32)
        for j in range(1, n_p):
            g = g + p_ref[j].astype(F32)
        m_new = ADAM_B1 * m_ref[...] + (1.0 - ADAM_B1) * g
        v_new = ADAM_B2 * v_ref[...] + (1.0 - ADAM_B2) * jnp.square(g)
        g_out[...] = g
        m_out[...] = m_new
        v_out[...] = v_new
        d_out[...] = -ADAM_LR * ((m_new / c1) / (jnp.sqrt(v_new / c2) + ADAM_EPS) + ADAM_WD * w_ref[...])

    spec = pl.BlockSpec((tr, cols), lambda i: (i, 0))
    return pl.pallas_call(
        body, grid=(rows // tr,), in_specs=[spec, pl.BlockSpec((n_p, tr, cols), lambda i: (0, i, 0)), spec, spec],
        out_specs=[spec] * 4, out_shape=[SDS((rows, cols), F32)] * 4, compiler_params=_cp("parallel"), name=name)(w, parts, m, v)


FWD_NAMES = ["x", "c", "ctx", "c_ctx", "ada_w", "ada_b", "norm_g", "mlp_w1", "mlp_w2", "final_g", "ssd_w_in", "ssd_conv_w",
             "ssd_conv_b", "ssd_dt_bias", "ssd_a_log", "ssd_d", "ssd_norm_g", "ssd_w_out", "ret_w_in", "ret_log_decay",
             "ret_w_out", "hgrn_w_in", "hgrn_lb_logits", "hgrn_norm_g", "hgrn_w_out", "gdn_w_in", "gdn_conv_w", "gdn_dt_bias",
             "gdn_a_log", "gdn_norm_g", "gdn_w_out"]
WEIGHTS = FWD_NAMES[3:]
BIG = {"mlp_w1": 2, "mlp_w2": 1, "ssd_w_in": 2, "ssd_w_out": 1, "ret_w_in": 2, "ret_w_out": 1, "hgrn_w_in": 2,
       "hgrn_w_out": 1, "gdn_w_in": 2, "gdn_w_out": 1}
SMALL_SHARDED = ["norm_g", "ssd_conv_w", "gdn_conv_w", "hgrn_norm_g"]
SMALL = [n for n in WEIGHTS if n not in BIG and n != "ada_w"]


def _to_full(g8, axis):
    _, l, a, b = g8.shape
    if axis == 1:
        return g8.transpose(1, 0, 2, 3).reshape(l, N_DEV * a, b)
    return g8.transpose(1, 2, 0, 3).reshape(l, a, N_DEV * b)


def _to_shards(full, axis):
    l, a, b = full.shape
    if axis == 1:
        return full.reshape(l, N_DEV, a // N_DEV, b).transpose(1, 0, 2, 3)
    return full.reshape(l, a, N_DEV, b // N_DEV).transpose(2, 0, 1, 3)


def _pack_small(arrs):
    parts, meta, off = [], [], 0
    for a in arrs:
        n = a.size
        npad = -(-n // LANES) * LANES
        parts.append(jnp.pad(a.reshape(-1).astype(F32), (0, npad - n)))
        meta.append((off, n, a.shape))
        off += npad
    rows = -(-(off // LANES) // 8) * 8
    flat = jnp.concatenate(parts)
    flat = jnp.pad(flat, (0, rows * LANES - off))
    return flat.reshape(rows, LANES), meta


def _unpack_small(buf, meta):
    flat = buf.reshape(buf.shape[:-2] + (-1,))
    return [flat[..., off:off + n].reshape(buf.shape[:-2] + tuple(shape)) for off, n, shape in meta]


def _my_shard(full, me, n_local):
    return lax.dynamic_slice_in_dim(full, me * n_local, n_local, axis=full.ndim - 1)


def kernel(*args):
    n_f = len(FWD_NAMES)
    inp = dict(zip(FWD_NAMES, args[:n_f]))
    tgt = args[n_f][0]
    n_w = len(WEIGHTS)
    mom_m = dict(zip(WEIGHTS, args[n_f + 1:n_f + 1 + n_w]))
    mom_v = dict(zip(WEIGHTS, args[n_f + 1 + n_w:n_f + 1 + 2 * n_w]))
    x, ctx, c = inp["x"][0], inp["ctx"][0], inp["c"]
    d = x.shape[1]
    me = 4 * lax.axis_index("x") + 2 * lax.axis_index("y") + lax.axis_index("c")

    buf, meta = _pack_small([c] + [inp[n] for n in SMALL_SHARDED])
    got = _unpack_small(exchange(buf, "gather_small"), meta)
    c_all = got[0].reshape(N_DEV, d)
    small_full = {}
    for n, g8 in zip(SMALL_SHARDED, got[1:]):
        small_full[n] = jnp.moveaxis(g8, 0, -2).reshape(g8.shape[1:-1] + (N_DEV * g8.shape[-1],))
    cond_in = jnp.concatenate([c_all, inp["c_ctx"][None]], axis=0)
    cond = _silu(cond_in)
    cond16 = jnp.pad(cond, ((0, 16 - cond.shape[0]), (0, 0)))

    ada_w = inp["ada_w"]
    n_ada = ada_w.shape[2]
    mod_loc = jnp.stack([_mm(cond16, ada_w[i], "nn", "ada_fwd")[:N_DEV + 1] for i in range(DEPTH)])
    mbuf, mmeta = _pack_small([mod_loc])
    (mod8,) = _unpack_small(exchange(mbuf, "gather_mod"), mmeta)
    mod_full = mod8.transpose(1, 2, 0, 3).reshape(DEPTH, N_DEV + 1, N_DEV * n_ada) + inp["ada_b"][:, None, :]
    mod2 = jnp.stack([mod_full[:, N_DEV], lax.dynamic_index_in_dim(mod_full, me, axis=1, keepdims=False)], axis=1)

    big = list(BIG)
    wg = gather_two_level([inp[n].astype(BF16).reshape(-1, inp[n].shape[2]) for n in big], "gather_weights")
    p = {}
    for n, g8 in zip(big, wg):
        p[n] = _to_full(g8.reshape((N_DEV,) + inp[n].shape), BIG[n])
    p["gdn_w_in_main"], p["gdn_w_in_gate"] = p["gdn_w_in"][0, :, :6 * d], p["gdn_w_in"][0, :, 6 * d:]
    for n in ("ssd_w_in", "ssd_w_out", "ret_w_in", "ret_w_out", "hgrn_w_in", "hgrn_w_out", "gdn_w_out"):
        p[n] = p[n][0]
    del p["gdn_w_in"]
    for n in SMALL:
        if n not in ("c_ctx", "ada_b"):
            p[n] = small_full[n] if n in small_full else inp[n]
    for n in ("ssd_conv_w", "ssd_conv_b", "ssd_dt_bias", "ssd_a_log", "ssd_d", "ssd_norm_g", "ret_log_decay", "hgrn_norm_g",
              "gdn_conv_w", "gdn_dt_bias", "gdn_a_log", "gdn_norm_g"):
        p[n] = p[n][0]

    loss_loc, (g_mod2, g_x, g_p) = jax.value_and_grad(
        lambda mod2_, x_, p_: local_loss(x_, ctx, mod2_, tgt, p_), argnums=(0, 1, 2))(mod2, x, p)

    small_g_names = [n for n in SMALL if n not in ("c_ctx", "ada_b")]
    gbuf, gmeta = _pack_small([loss_loc.reshape(1), g_mod2] + [g_p[n] for n in small_g_names])
    g8 = exchange(gbuf, "gather_small_grads")
    gsum = _unpack_small(sum_parts(g8, "sum_small_grads"), gmeta)
    loss = gsum[0][0]
    g_small = dict(zip(small_g_names, gsum[2:]))
    for n in ("ssd_conv_w", "ssd_conv_b", "ssd_dt_bias", "ssd_a_log", "ssd_d", "ssd_norm_g", "ret_log_decay", "hgrn_norm_g",
              "gdn_conv_w", "gdn_dt_bias", "gdn_a_log", "gdn_norm_g"):
        g_small[n] = g_small[n][None]
    dmod_each = _unpack_small(g8, gmeta)[1]
    dmod9 = jnp.concatenate([dmod_each[:, :, 1].transpose(1, 0, 2), gsum[1][:, 0:1]], axis=1)
    g_small["ada_b"] = gsum[1][:, 0] + gsum[1][:, 1]
    dmod16 = jnp.pad(_my_shard(dmod9, me, n_ada), ((0, 0), (0, 16 - dmod9.shape[1]), (0, 0)))
    g_ada_w = jnp.stack([_mm(cond16, dmod16[i], "tn", "ada_dw") for i in range(DEPTH)])
    dcond_part = _mm(dmod16[0], ada_w[0], "nt", "ada_dx")
    for i in range(1, DEPTH):
        dcond_part = dcond_part + _mm(dmod16[i], ada_w[i], "nt", "ada_dx")
    cbuf, cmeta = _pack_small([dcond_part[N_DEV]])
    (dcond8,) = _unpack_small(sum_parts(exchange(cbuf, "gather_dcond"), "sum_dcond"), cmeta)
    g_small["c_ctx"] = jax.vjp(_silu, inp["c_ctx"])[1](dcond8)[0]

    g_p["gdn_w_in"] = jnp.concatenate([g_p.pop("gdn_w_in_main"), g_p.pop("gdn_w_in_gate")], axis=1)
    my_core = lax.axis_index("c")
    blocks = []
    for n in big:
        gf = g_p[n] if g_p[n].ndim == 3 else g_p[n][None]
        blocks.append(_to_shards(gf, BIG[n]).reshape(4, 2, -1, inp[n].shape[2]))
    from_sibling = scatter_sibling(blocks, "scatter_grads_sibling")
    sums = [pair_sum(lax.dynamic_index_in_dim(b, my_core, axis=1, keepdims=False), t, "pair_sum_" + n)
            for n, b, t in zip(big, blocks, from_sibling)]
    parts = scatter_chips(sums, "scatter_grads_chips")

    out = {}

    def view(a):
        return a.reshape(-1, a.shape[-1])

    for n, pr in zip(big, parts):
        res = adamw(view(inp[n]), pr, view(mom_m[n]), view(mom_v[n]), "adamw_" + n)
        out[n] = [r.reshape(inp[n].shape) for r in res]
    res = adamw(view(ada_w), view(g_ada_w)[None], view(mom_m["ada_w"]), view(mom_v["ada_w"]), "adamw_ada_w")
    out["ada_w"] = [r.reshape(ada_w.shape) for r in res]
    for n in SMALL_SHARDED:
        g_small[n] = _my_shard(g_small[n], me, inp[n].shape[-1])
    wb, wmeta = _pack_small([inp[n] for n in SMALL])
    gb, _ = _pack_small([g_small[n] for n in SMALL])
    mb, _ = _pack_small([mom_m[n] for n in SMALL])
    vb, _ = _pack_small([mom_v[n] for n in SMALL])
    res = [_unpack_small(r, wmeta) for r in adamw(wb, gb[None], mb, vb, "adamw_small")]
    for j, n in enumerate(SMALL):
        out[n] = [r[j] for r in res]

    outs = [loss, g_x[None]]
    for k in range(4):
        outs += [out[n][k] for n in WEIGHTS]
    return tuple(outs)
```

```python
import functools
import math

import jax
import jax.numpy as jnp
from jax import lax
from jax.experimental import pallas as pl
from jax.experimental.pallas import tpu as pltpu

F32 = jnp.float32
BF16 = jnp.bfloat16
HI = lax.Precision.HIGHEST

N_DEV = 8
CHUNK = 64
SUB = 16
GROUPS_PER_STEP = 2
GRID_W = 64
ROPE_BASE = 10000.0
NORM_EPS = 1e-6
DEPTH = 4
SSD_GROUPS = 8
SSD_STATE = 128
SSD_HEAD_DIM = 64
RET_HEADS = 8
HEAD128 = 128
ADAM_LR, ADAM_B1, ADAM_B2, ADAM_EPS, ADAM_WD, ADAM_STEP = 0.001, 0.9, 0.999, 1e-08, 0.01, 10

V7X_VMEM_BYTES = 64 * 1024 * 1024
VMEM_LIMIT = (V7X_VMEM_BYTES * 3) // 4
ROW_PIPELINE_BYTES = V7X_VMEM_BYTES // 4
LANES = 128
MESH = pl.DeviceIdType.MESH

SDS = jax.ShapeDtypeStruct


def _cp(*sem):
    return pltpu.CompilerParams(dimension_semantics=tuple(sem), vmem_limit_bytes=VMEM_LIMIT)


def _pick(n, cap, quantum=LANES):
    best = None
    d = quantum
    while d <= min(n, cap):
        if n % d == 0:
            best = d
        d += quantum
    return n if best is None else best


def _mm(a, b, mode, name, out_dtype=F32):
    if mode == "nn":
        (m, k), n = a.shape, b.shape[1]
    elif mode == "nt":
        (m, k), n = a.shape, b.shape[0]
    else:
        (k, m), n = a.shape, b.shape[1]
    tm, tn, tk = _pick(m, 1024), _pick(n, 1280), _pick(k, 2048)
    nk = k // tk
    if mode == "nn":
        a_spec = pl.BlockSpec((tm, tk), lambda i, j, kk: (i, kk))
        b_spec = pl.BlockSpec((tk, tn), lambda i, j, kk: (kk, j))
        dims = (((1,), (0,)), ((), ()))
    elif mode == "nt":
        a_spec = pl.BlockSpec((tm, tk), lambda i, j, kk: (i, kk))
        b_spec = pl.BlockSpec((tn, tk), lambda i, j, kk: (j, kk))
        dims = (((1,), (1,)), ((), ()))
    else:
        a_spec = pl.BlockSpec((tk, tm), lambda i, j, kk: (kk, i))
        b_spec = pl.BlockSpec((tk, tn), lambda i, j, kk: (kk, j))
        dims = (((0,), (0,)), ((), ()))

    def body(a_ref, b_ref, o_ref, acc_ref):
        kk = pl.program_id(2)

        @pl.when(kk == 0)
        def _():
            acc_ref[...] = jnp.zeros_like(acc_ref)

        acc_ref[...] += lax.dot_general(a_ref[...].astype(BF16), b_ref[...].astype(BF16), dims,
                                        preferred_element_type=F32)

        @pl.when(kk == nk - 1)
        def _():
            o_ref[...] = acc_ref[...].astype(o_ref.dtype)

    return pl.pallas_call(
        body, grid=(m // tm, n // tn, nk), in_specs=[a_spec, b_spec],
        out_specs=pl.BlockSpec((tm, tn), lambda i, j, kk: (i, j)),
        out_shape=SDS((m, n), out_dtype), scratch_shapes=[pltpu.VMEM((tm, tn), F32)],
        compiler_params=_cp("parallel", "parallel", "arbitrary"), name=name)(a, b)


def pmatmul(a, w, name):
    @jax.custom_vjp
    def op(a, w):
        return _mm(a, w, "nn", name + "_fwd")

    def fwd(a, w):
        return _mm(a, w, "nn", name + "_fwd"), (a, w)

    def bwd(res, g):
        a, w = res
        return _mm(g, w, "nt", name + "_dx"), _mm(a, g, "tn", name + "_dw", out_dtype=w.dtype)

    op.defvjp(fwd, bwd)
    return op(a, w)


def rowop(f, name, rows, pars, out_ws, tile, ctx_rows=0):
    t = rows[0].shape[0]
    tile = math.gcd(math.gcd(tile, t), ctx_rows if ctx_rows else t)
    row_bytes = 2 * 4 * (2 * sum(r.shape[1] for r in rows) + sum(out_ws))
    while tile > 8 and tile % 2 == 0 and tile * row_bytes > ROW_PIPELINE_BYTES:
        tile //= 2
    nt = t // tile
    cut = ctx_rows // tile
    n_r, n_p, n_o = len(rows), len(pars), len(out_ws)
    segs = [p.shape[0] for p in pars]
    assert all(s in (1, 2) for s in segs)

    def par_map(s):
        if s == 1:
            return lambda i: (0, 0, 0)
        return lambda i: (jnp.where(i >= cut, 1, 0), 0, 0)

    row_specs = [pl.BlockSpec((tile, r.shape[1]), lambda i: (i, 0)) for r in rows]
    par_specs = [pl.BlockSpec((None, 1, p.shape[2]), par_map(p.shape[0])) for p in pars]
    out_specs = [pl.BlockSpec((tile, w), lambda i: (i, 0)) for w in out_ws]
    out_shape = [SDS((t, w), F32) for w in out_ws]

    def fwd_call(*args):
        def body(*refs):
            outs = f(*[r[...] for r in refs[:n_r + n_p]])
            for o_ref, o in zip(refs[n_r + n_p:], outs):
                o_ref[...] = o

        return tuple(pl.pallas_call(body, grid=(nt,), in_specs=row_specs + par_specs, out_specs=out_specs,
                                    out_shape=out_shape, compiler_params=_cp("parallel"), name=name + "_fwd")(*args))

    def bwd_call(args, gouts):
        def body(*refs):
            ins = [r[...] for r in refs[:n_r + n_p]]
            gs = tuple(r[...] for r in refs[n_r + n_p:n_r + n_p + n_o])
            d_refs = refs[n_r + n_p + n_o:]
            _, vjp = jax.vjp(f, *ins)
            grads = vjp(gs)
            for ref, g in zip(d_refs[:n_r], grads[:n_r]):
                ref[...] = g
            i = pl.program_id(0)
            for ref, g, s in zip(d_refs[n_r:], grads[n_r:], segs):
                first = (i == 0) if s == 1 else jnp.logical_or(i == 0, i == cut)

                @pl.when(first)
                def _(ref=ref, g=g):
                    ref[...] = g

                @pl.when(jnp.logical_not(first))
                def _(ref=ref, g=g):
                    ref[...] += g

        d_specs = row_specs + par_specs
        d_shape = [SDS(r.shape, F32) for r in rows] + [SDS(p.shape, F32) for p in pars]
        return tuple(pl.pallas_call(body, grid=(nt,), in_specs=row_specs + par_specs + out_specs, out_specs=d_specs,
                                    out_shape=d_shape, compiler_params=_cp("arbitrary"), name=name + "_bwd")(*args, *gouts))

    @jax.custom_vjp
    def op(*args):
        return fwd_call(*args)

    op.defvjp(lambda *args: (fwd_call(*args), args), bwd_call)
    return op(*rows, *pars)


def scanop(chunk_fn, name, cols, col_ws, grps, state_shape, y_w, rev=False, ctx_rows=0, unroll=1, lockstep=False):
    t = cols[0].shape[0]
    g_n = cols[0].shape[1] // col_ws[0]
    u_n = unroll if g_n % unroll == 0 else 1
    gs_n = g_n // u_n
    nc = t // CHUNK
    ncx = ctx_rows // CHUNK
    n_c, n_g = len(cols), len(grps)

    def order(c):
        if not rev:
            return c
        return jnp.where(c < ncx, ncx - 1 - c, nc - 1 - (c - ncx))

    def specs(cmap):
        col_specs = [pl.BlockSpec((CHUNK, u_n * w), lambda g, c: (cmap(c), g)) for w in col_ws]
        grp_specs = [pl.BlockSpec((u_n, CHUNK, a.shape[2]), lambda g, c: (g, cmap(c), 0)) for a in grps]
        y_spec = pl.BlockSpec((CHUNK, u_n * y_w), lambda g, c: (cmap(c), g))
        s_spec = pl.BlockSpec((u_n, None) + state_shape, lambda g, c: (g, cmap(c), 0, 0))
        return col_specs, grp_specs, y_spec, s_spec

    def group_ins(ins, u):
        return ([a[:, u * w:(u + 1) * w] for a, w in zip(ins[:n_c], col_ws)] + [a[u] for a in ins[n_c:]])

    if lockstep:
        step_fn = chunk_fn
    else:
        def step_fn(s_list, ins_list):
            return [chunk_fn(s, *ins) for s, ins in zip(s_list, ins_list)]

    def fwd_call(*args):
        col_specs, grp_specs, y_spec, s_spec = specs(order)

        def body(*refs):
            ins = [r[...] for r in refs[:n_c + n_g]]
            y_ref, sall_ref, s_ref = refs[n_c + n_g:]

            @pl.when(pl.program_id(1) == 0)
            def _():
                s_ref[...] = jnp.zeros_like(s_ref)

            s = s_ref[...]
            sall_ref[...] = s
            res = step_fn([s[u] for u in range(u_n)], [group_ins(ins, u) for u in range(u_n)])
            y_ref[...] = jnp.concatenate([y for _, y in res], axis=1)
            s_ref[...] = jnp.stack([s_new for s_new, _ in res])

        return pl.pallas_call(
            body, grid=(gs_n, nc), in_specs=col_specs + grp_specs, out_specs=[y_spec, s_spec],
            out_shape=[SDS((t, g_n * y_w), F32), SDS((g_n, nc) + state_shape, F32)],
            scratch_shapes=[pltpu.VMEM((u_n,) + state_shape, F32)],
            compiler_params=_cp("parallel", "arbitrary"), name=name + "_fwd")(*args)

    def bwd_call(res, gy):
        args, s_all = res
        col_specs, grp_specs, y_spec, s_spec = specs(lambda c: order(nc - 1 - c))

        def body(*refs):
            ins = [r[...] for r in refs[:n_c + n_g]]
            s_prev = refs[n_c + n_g][...]
            dy = refs[n_c + n_g + 1][...]
            d_refs = refs[n_c + n_g + 2:-1]
            ds_ref = refs[-1]

            @pl.when(pl.program_id(1) == 0)
            def _():
                ds_ref[...] = jnp.zeros_like(ds_ref)

            ds = ds_ref[...]
            _, vjp = jax.vjp(step_fn, [s_prev[u] for u in range(u_n)], [group_ins(ins, u) for u in range(u_n)])
            d_s, d_ins = vjp([(ds[u], dy[:, u * y_w:(u + 1) * y_w]) for u in range(u_n)])
            grads = [[d_s[u]] + list(d_ins[u]) for u in range(u_n)]
            ds_ref[...] = jnp.stack([g[0] for g in grads])
            for j, ref in enumerate(d_refs):
                ref[...] = (jnp.concatenate([g[1 + j] for g in grads], axis=1) if j < n_c
                            else jnp.stack([g[1 + j] for g in grads]))

        return tuple(pl.pallas_call(
            body, grid=(gs_n, nc), in_specs=col_specs + grp_specs + [s_spec, y_spec], out_specs=col_specs + grp_specs,
            out_shape=[SDS(a.shape, F32) for a in args],
            scratch_shapes=[pltpu.VMEM((u_n,) + state_shape, F32)],
            compiler_params=_cp("parallel", "arbitrary"), name=name + "_bwd")(*args, s_all, gy))

    @jax.custom_vjp
    def op(*args):
        return fwd_call(*args)[0]

    def fwd(*args):
        y, s_all = fwd_call(*args)
        return y, (args, s_all)

    op.defvjp(fwd, bwd_call)
    return op(*cols, *grps)


def _dg(a, b, ca, cb, prec=None):
    return lax.dot_general(a, b, (((ca,), (cb,)), ((), ())), precision=prec, preferred_element_type=F32)


@functools.partial(jax.custom_vjp, nondiff_argnums=(2, 3))
def mmb(a, b, ca, cb):
    return _dg(a.astype(BF16), b.astype(BF16), ca, cb)


def _mmb_fwd(a, b, ca, cb):
    return mmb(a, b, ca, cb), (a, b)


def _mmb_bwd(ca, cb, res, g):
    a, b = res
    ab, bb, gb = a.astype(BF16), b.astype(BF16), g.astype(BF16)
    da = _dg(gb, bb, 1, 1 - cb) if ca == 1 else _dg(bb, gb, 1 - cb, 1)
    db = _dg(ab, gb, 1 - ca, 0) if cb == 0 else _dg(gb, ab, 0, 1 - ca)
    return da, db


mmb.defvjp(_mmb_fwd, _mmb_bwd)


def mmh(a, b, ca=1, cb=0):
    return _dg(a, b, ca, cb, HI)


def _masks(n, rev=False):
    i = lax.broadcasted_iota(jnp.int32, (n, n), 0)
    j = lax.broadcasted_iota(jnp.int32, (n, n), 1)
    return (i <= j, i < j) if rev else (i >= j, i > j)


def _silu(x):
    return x * jax.nn.sigmoid(x)


def _softplus(x):
    return jnp.maximum(x, 0.0) + jnp.log(1.0 + jnp.exp(-jnp.abs(x)))


def _expand_heads(p, width):
    h = p.shape[1]
    lane = lax.broadcasted_iota(jnp.int32, (h, h * width), 1)
    row = lax.broadcasted_iota(jnp.int32, (h, h * width), 0)
    e = jnp.where(lane // width == row, 1.0, 0.0).astype(F32)
    return mmh(jnp.broadcast_to(p, (8, h)), e)[0:1]


def _group_apply(x, width, fn):
    n = x.shape[1] // width
    return jnp.concatenate([fn(x[:, g * width:(g + 1) * width]) for g in range(n)], axis=1)


def _rms(x):
    return x * lax.rsqrt(jnp.mean(x * x, axis=-1, keepdims=True) + NORM_EPS)


@jax.custom_vjp
def unit_lower_inv(ls):
    n = ls[0].shape[0]
    eye = jnp.where(_masks(n)[0] & jnp.logical_not(_masks(n)[1]), 1.0, 0.0).astype(F32)
    ps = [-l for l in ls]
    xs = [eye + p for p in ps]
    for _ in range(int(math.log2(n)) - 1):
        ps = [mmh(p, p) for p in ps]
        xs = [x + mmh(x, p) for x, p in zip(xs, ps)]
    return tuple(xs)


def _uli_fwd(ls):
    xs = unit_lower_inv(ls)
    return xs, xs


def _uli_bwd(xs, gs):
    ts = [_dg(x, g, 0, 0, HI) for x, g in zip(xs, gs)]
    return (tuple(-_dg(t, x, 1, 1, HI) for t, x in zip(ts, xs)),)


unit_lower_inv.defvjp(_uli_fwd, _uli_bwd)


def scalar_decay_chunk(r_n, n, p, with_dt, rev):
    def fn(s, q, k, v, la, *rest):
        c = q.shape[0]
        incl, _ = _masks(c, rev)
        tril = incl.astype(F32)
        cum = mmh(tril, la)
        cum_t = _dg(la, tril, 0, 1, HI)
        tot = cum[0:1, :] if rev else cum[c - 1:c, :]
        scores = mmb(q, k, 1, 1)
        ys, ss = [], []
        for r in range(r_n):
            cr = cum[:, r:r + 1]
            dec = jnp.exp(jnp.where(incl, cr - cum_t[r:r + 1, :], -jnp.inf))
            vr = v[:, r * p:(r + 1) * p]
            if with_dt:
                vr = vr * rest[0][:, r:r + 1]
            sr = s[r * n:(r + 1) * n, :]
            ys.append(mmb(scores * dec, vr, 1, 0) + mmb(q, sr, 1, 0) * jnp.exp(cr))
            ss.append(jnp.exp(tot[:, r:r + 1]) * sr + mmb(k, jnp.exp(tot[:, r:r + 1] - cr) * vr, 0, 0))
        return jnp.concatenate(ss, axis=0), jnp.concatenate(ys, axis=1)

    return fn


def vector_decay_chunk(rev):
    return functools.partial(_vector_decay_chunk, rev)


def _vector_decay_chunk(rev, s, q, k, v, lf):
    c, kd = q.shape
    incl, _ = _masks(c, rev)
    cum = mmh(incl.astype(F32), lf)
    tot = cum[0:1, :] if rev else cum[c - 1:c, :]
    tot_col = _dg(lf, jnp.ones((c, 1), F32), 0, 0, HI)
    y_inter = mmb(q * jnp.exp(cum), s, 1, 0)
    s_new = jnp.exp(tot_col) * s + mmb(k * jnp.exp(tot - cum), v, 0, 0)
    i3 = lax.broadcasted_iota(jnp.int32, (SUB, SUB, 1), 0)
    j3 = lax.broadcasted_iota(jnp.int32, (SUB, SUB, 1), 1)
    ys = []
    for b in range(c // SUB):
        lo, hi = b * SUB, (b + 1) * SUB
        qb, kb, cb, vb = q[lo:hi], k[lo:hi], cum[lo:hi], v[lo:hi]
        dec = jnp.exp(jnp.where((i3 <= j3) if rev else (i3 >= j3), cb[:, None, :] - cb[None, :, :], -jnp.inf))
        a_diag = jnp.sum(qb[:, None, :] * kb[None, :, :] * dec, axis=-1)
        yb = mmb(a_diag, vb, 1, 0)
        if rev and hi < c:
            base = cum[hi:hi + 1, :]
            a_off = mmb(qb * jnp.exp(cb - base), k[hi:] * jnp.exp(base - cum[hi:]), 1, 1)
            yb = yb + mmb(a_off, v[hi:], 1, 0)
        if not rev and b > 0:
            base = cum[lo - 1:lo, :]
            a_off = mmb(qb * jnp.exp(cb - base), k[:lo] * jnp.exp(base - cum[:lo]), 1, 1)
            yb = yb + mmb(a_off, v[:lo], 1, 0)
        ys.append(yb)
    return s_new, y_inter + jnp.concatenate(ys, axis=0)


def delta_chunk(r_n, kd, vd, rev):
    def fn(s_list, ins_list):
        c = ins_list[0][0].shape[0]
        incl, strict = _masks(c, rev)
        tril = incl.astype(F32)
        heads = [(g, r) for g in range(len(ins_list)) for r in range(r_n)]
        q = [ins[0] for ins in ins_list]
        k = [ins[1] for ins in ins_list]
        cum = [mmh(tril, ins[4]) for ins in ins_list]
        cum_t = [_dg(ins[4], tril, 0, 1, HI) for ins in ins_list]
        kk = [mmb(a, a, 1, 1) for a in k]
        qk = [mmb(a, b, 1, 1) for a, b in zip(q, k)]
        cr = [cum[g][:, r:r + 1] for g, r in heads]
        br = [ins_list[g][3][:, r:r + 1] for g, r in heads]
        seg = [c_ - cum_t[g][r:r + 1, :] for c_, (g, r) in zip(cr, heads)]
        tot = [(cum[g][0:1, :] if rev else cum[g][c - 1:c, :])[:, r:r + 1] for g, r in heads]
        vr = [ins_list[g][2][:, r * vd:(r + 1) * vd] for g, r in heads]
        sr = [s_list[g][r * kd:(r + 1) * kd, :] for g, r in heads]
        xs = unit_lower_inv(tuple(b * kk[g] * jnp.exp(jnp.where(strict, sg, -jnp.inf))
                                  for b, sg, (g, r) in zip(br, seg, heads)))
        u = [mmh(x, v_ * b) for x, v_, b in zip(xs, vr, br)]
        w = [mmh(x, k[g] * (b * jnp.exp(c_))) for x, b, c_, (g, r) in zip(xs, br, cr, heads)]
        ws = [mmb(w_, s_, 1, 0) for w_, s_ in zip(w, sr)]
        v_new = [u_ - a for u_, a in zip(u, ws)]
        ya = [mmb(qk[g] * jnp.exp(jnp.where(incl, sg, -jnp.inf)), vn, 1, 0) for sg, vn, (g, r) in zip(seg, v_new, heads)]
        yb = [mmb(q[g], s_, 1, 0) * jnp.exp(c_) for s_, c_, (g, r) in zip(sr, cr, heads)]
        sn = [jnp.exp(t_) * s_ + mmb(k[g], jnp.exp(t_ - c_) * vn, 0, 0)
              for t_, s_, c_, vn, (g, r) in zip(tot, sr, cr, v_new, heads)]
        out = []
        for g in range(len(ins_list)):
            idx = [i for i, (g2, _) in enumerate(heads) if g2 == g]
            out.append((jnp.concatenate([sn[i] for i in idx], axis=0), jnp.concatenate([ya[i] + yb[i] for i in idx], axis=1)))
        return out

    return fn


def f_adaln(x, g, sh, sc):
    return ((_rms(x) * g) * (1.0 + sc) + sh,)


def f_resid(x, y, gate):
    return (x + gate * y,)


def f_relu2(h):
    return (jnp.square(jnp.maximum(h, 0.0)),)


def f_conv_bias(up, u, un, w0, w1, w2, b):
    return (_silu(w0 * up + w1 * u + w2 * un + b),)


def f_conv(up, u, un, w0, w1, w2):
    return (_silu(w0 * up + w1 * u + w2 * un),)


def f_dt(raw, bias, a_log):
    dt = _softplus(raw + bias)
    return dt, -jnp.exp(a_log) * dt


def f_ssd_post(yf, yb, xs, z, d_skip, ng):
    y = (yf + yb + _expand_heads(d_skip, SSD_HEAD_DIM) * xs) * _silu(z)
    return (_group_apply(y, y.shape[1] // SSD_GROUPS, _rms) * ng,)


def f_rope(scale):
    def f(q, qsw, cos, sin):
        return ((q * cos + qsw * sin) * scale,)
    return f


def f_ret_post(yf, yb, g):
    def ln(y):
        mu = jnp.mean(y, axis=-1, keepdims=True)
        var = jnp.mean(jnp.square(y - mu), axis=-1, keepdims=True)
        return (y - mu) * lax.rsqrt(var + NORM_EPS)
    y = yf + yb
    return (_group_apply(y, y.shape[1] // RET_HEADS, ln) * _silu(g),)


def f_lower_bound(layer):
    def f(logits):
        m = jnp.max(logits, axis=0, keepdims=True)
        e = jnp.exp(logits - m)
        p = e / jnp.sum(e, axis=0, keepdims=True)
        lb = p[1:2]
        for j in range(2, layer + 1):
            lb = lb + p[j:j + 1]
        return (jnp.broadcast_to(lb, logits.shape),)
    return f


def f_hgrn_gates(ff, fb, lb):
    def gates(f):
        log_sig = jnp.minimum(f, 0.0) - jnp.log(1.0 + jnp.exp(-jnp.abs(f)))
        a, b = jnp.log(lb), jnp.log(1.0 - lb) + log_sig
        m = jnp.maximum(a, b)
        return m + jnp.log(jnp.exp(a - m) + jnp.exp(b - m)), (1.0 - lb) * jax.nn.sigmoid(-f)
    lf_f, k_f = gates(ff)
    lf_b, k_b = gates(fb)
    return lf_f, k_f, lf_b, k_b


def f_hgrn_post(yf, yb, g, ng):
    return (_group_apply(yf + yb, HEAD128, _rms) * ng * _silu(g),)


def f_gdn_qk(q, k):
    def l2(x):
        return x * lax.rsqrt(jnp.sum(x * x, axis=-1, keepdims=True) + 1e-6)
    return _group_apply(q, HEAD128, l2) * HEAD128 ** -0.5, _group_apply(k, HEAD128, l2)


def f_gdn_gates(bt, a, dt_bias, a_log):
    return jax.nn.sigmoid(bt), -jnp.exp(a_log) * _softplus(a + dt_bias)


def f_gdn_post(yf, yb, z, ng):
    n = yf.shape[1] // HEAD128
    return (_group_apply(yf + yb, HEAD128, _rms) * jnp.concatenate([ng] * n, axis=1) * _silu(z),)


def f_loss(x, tgt, fg):
    err = jnp.square(_rms(x) * fg - tgt)
    return (jnp.broadcast_to(jnp.mean(err, axis=-1, keepdims=True), (x.shape[0], LANES)),)


def _par(v):
    return v.reshape(1, 1, -1)


def _shift(u, lc):
    z = jnp.zeros((1, u.shape[1]), u.dtype)
    prev = jnp.concatenate([z, u[:lc - 1], z, u[lc:-1]], axis=0)
    nxt = jnp.concatenate([u[1:lc], z, u[lc + 1:], z], axis=0)
    return prev, nxt


def _grp(a, g_n):
    t = a.shape[0]
    return a.reshape(t, g_n, -1).transpose(1, 0, 2)


def ssd_mixer(h, p, lc, start):
    d = h.shape[1]
    di = 2 * d
    gn = SSD_GROUPS * SSD_STATE
    heads = di // SSD_HEAD_DIM
    r_n = heads // SSD_GROUPS
    u = pmatmul(h, p["ssd_w_in"], "ssd_in")
    z, xbc, dtr = u[:, :di], u[:, di:di + di + 2 * gn], u[:, 2 * di + 2 * gn:]
    up, un = _shift(xbc, lc)
    cw = p["ssd_conv_w"]
    (xbc,) = rowop(f_conv_bias, "ssd_conv", [up, xbc, un], [_par(cw[0]), _par(cw[1]), _par(cw[2]), _par(p["ssd_conv_b"])],
                   [xbc.shape[1]], 128)
    xs, bm, cm = xbc[:, :di], xbc[:, di:di + gn], xbc[:, di + gn:]
    dt, la = rowop(f_dt, "ssd_dt", [dtr], [_par(p["ssd_dt_bias"]), _par(p["ssd_a_log"])], [2 * heads, 2 * heads], 256)

    def scan(tag, rev, grps):
        return scanop(scalar_decay_chunk(r_n, SSD_STATE, SSD_HEAD_DIM, True, rev), "ssd_scan_" + tag, [cm, bm, xs],
                      [SSD_STATE, SSD_STATE, r_n * SSD_HEAD_DIM], grps, (r_n * SSD_STATE, SSD_HEAD_DIM),
                      r_n * SSD_HEAD_DIM, rev, lc)

    yf = scan("f", False, [_grp(la[:, :heads], SSD_GROUPS), _grp(dt[:, :heads], SSD_GROUPS)])
    yb = scan("b", True, [_grp(la[:, heads:], SSD_GROUPS), _grp(dt[:, heads:], SSD_GROUPS)])
    (yn,) = rowop(f_ssd_post, "ssd_post", [yf[start:], yb[start:], xs[start:], z[start:]],
                  [_par(p["ssd_d"]), _par(p["ssd_norm_g"])], [di], 128)
    return pmatmul(yn, p["ssd_w_out"], "ssd_out")


def _rope_tables(n_lat, lc, d, heads):
    qk = d // heads
    half = qk // 2
    quarter = half // 2
    pos = jnp.arange(n_lat)
    inv_freq = ROPE_BASE ** (-jnp.arange(0, half, 2, dtype=F32) / half)

    def tab(p):
        ang = p.astype(F32)[:, None] * inv_freq
        return jnp.cos(ang), jnp.sin(ang)

    cr, sr = tab(pos // GRID_W)
    cc, sc = tab(pos % GRID_W)
    cos = jnp.concatenate([cr, cr, cc, cc], axis=1)
    sin = jnp.concatenate([-sr, sr, -sc, sc], axis=1)
    cos = jnp.concatenate([jnp.ones((lc, qk), F32), cos], axis=0)
    sin = jnp.concatenate([jnp.zeros((lc, qk), F32), sin], axis=0)
    return jnp.tile(cos, (1, heads)), jnp.tile(sin, (1, heads)), quarter


def retention_mixer(h, p, lc, start):
    t, d = h.shape
    dv = 2 * d
    qk = d // RET_HEADS
    u = pmatmul(h, p["ret_w_in"], "ret_in")
    q, k, v, g = u[:, :d], u[:, d:2 * d], u[:, 2 * d:2 * d + dv], u[:, 2 * d + dv:]
    cos, sin, quarter = _rope_tables(t - lc, lc, d, RET_HEADS)
    cos, sin = lax.stop_gradient(cos), lax.stop_gradient(sin)

    def swap(a):
        return a.reshape(t, -1, 2, quarter)[:, :, ::-1, :].reshape(t, d)

    (q,) = rowop(f_rope(1.0), "ret_rope_q", [q, swap(q), cos, sin], [], [d], 256)
    (k,) = rowop(f_rope(qk ** -0.5), "ret_rope_k", [k, swap(k), cos, sin], [], [d], 256)

    def scan(tag, rev, grps):
        return scanop(scalar_decay_chunk(1, qk, dv // RET_HEADS, False, rev), "ret_scan_" + tag, [q, k, v],
                      [qk, qk, dv // RET_HEADS], grps, (qk, dv // RET_HEADS), dv // RET_HEADS, rev, lc, GROUPS_PER_STEP)

    ld = p["ret_log_decay"]
    la_f = jnp.broadcast_to(ld[0][:, None, None], (RET_HEADS, t, 1))
    la_b = jnp.broadcast_to(ld[1][:, None, None], (RET_HEADS, t, 1))
    yf, yb = scan("f", False, [la_f]), scan("b", True, [la_b])
    (yn,) = rowop(f_ret_post, "ret_post", [yf[start:], yb[start:], g[start:]], [], [dv], 128)
    return pmatmul(yn, p["ret_w_out"], "ret_out")


def hgrn2_mixer(h, p, lc, start, layer):
    t, d = h.shape
    u = pmatmul(h, p["hgrn_w_in"], "hgrn_in")
    q, f_f, f_b, i, g = (u[:, j * d:(j + 1) * d] for j in range(5))
    (lb,) = rowop(f_lower_bound(layer), "hgrn_lb", [p["hgrn_lb_logits"]], [], [d], DEPTH)
    lf_f, k_f, lf_b, k_b = rowop(f_hgrn_gates, "hgrn_gates", [f_f, f_b], [lb[0:1][None]], [d] * 4, 256)

    def scan(tag, rev, cols):
        return scanop(vector_decay_chunk(rev), "hgrn_scan_" + tag, cols, [HEAD128] * 4, [], (HEAD128, HEAD128), HEAD128,
                      rev, lc, GROUPS_PER_STEP)

    yf = scan("f", False, [q, k_f, i, lf_f])
    yb = scan("b", True, [q, k_b, i, lf_b])
    (yn,) = rowop(f_hgrn_post, "hgrn_post", [yf[start:], yb[start:], g[start:]], [_par(p["hgrn_norm_g"])], [d], 256)
    return pmatmul(yn, p["hgrn_w_out"], "hgrn_out")


def gdn_mixer(h, p, lc, start):
    t, d = h.shape
    dk, dv = d, 2 * d
    kh = d // HEAD128
    r_n = 2
    cc = 2 * dk + dv
    u = pmatmul(h, p["gdn_w_in_main"], "gdn_in")
    ug = pmatmul(h, p["gdn_w_in_gate"], "gdn_in_gate")
    qkv, z = u[:, :cc], u[:, cc:]
    nb = 2 * kh * r_n
    bt, a = ug[:, :nb], ug[:, nb:]
    up, un = _shift(qkv, lc)
    cw = p["gdn_conv_w"]
    (qkv,) = rowop(f_conv, "gdn_conv", [up, qkv, un], [_par(cw[0]), _par(cw[1]), _par(cw[2])], [cc], 128)
    q, k, v = qkv[:, :dk], qkv[:, dk:2 * dk], qkv[:, 2 * dk:]
    q, k = rowop(f_gdn_qk, "gdn_qk", [q, k], [], [dk, dk], 256)
    beta, la = rowop(f_gdn_gates, "gdn_gates", [bt, a], [_par(p["gdn_dt_bias"]), _par(p["gdn_a_log"])], [nb, nb], 256)

    def scan(tag, rev, grps):
        return scanop(delta_chunk(r_n, HEAD128, HEAD128, rev), "gdn_scan_" + tag, [q, k, v],
                      [HEAD128, HEAD128, r_n * HEAD128], grps, (r_n * HEAD128, HEAD128), r_n * HEAD128, rev, lc,
                      GROUPS_PER_STEP, True)

    half = nb // 2
    yf = scan("f", False, [_grp(beta[:, :half], kh), _grp(la[:, :half], kh)])
    yb = scan("b", True, [_grp(beta[:, half:], kh), _grp(la[:, half:], kh)])
    (yn,) = rowop(f_gdn_post, "gdn_post", [yf[start:], yb[start:], z[start:]], [_par(p["gdn_norm_g"])], [dv], 128)
    return pmatmul(yn, p["gdn_w_out"], "gdn_out")


def local_loss(x, ctx, mod2, tgt, p):
    lc, d = ctx.shape
    xc = jnp.concatenate([ctx, x], axis=0)
    for i in range(DEPTH):
        keep = i < DEPTH - 1
        start = 0 if keep else lc
        sh1, sc1, g1, sh2, sc2, g2 = (mod2[i][:, j * d:(j + 1) * d][:, None, :] for j in range(6))
        (h,) = rowop(f_adaln, "adaln_a%d" % i, [xc], [_par(p["norm_g"][i, 0]), sh1, sc1], [d], 256, lc)
        if i % 4 == 0:
            y = ssd_mixer(h, p, lc, start)
        elif i % 4 == 1:
            y = retention_mixer(h, p, lc, start)
        elif i % 4 == 2:
            y = hgrn2_mixer(h, p, lc, start, i)
        else:
            y = gdn_mixer(h, p, lc, start)
        if not keep:
            xc, g1, sh2, sc2, g2, lc = xc[lc:], g1[1:], sh2[1:], sc2[1:], g2[1:], 0
        (xc,) = rowop(f_resid, "resid_a%d" % i, [xc, y], [g1], [d], 256, lc)
        (h2,) = rowop(f_adaln, "adaln_b%d" % i, [xc], [_par(p["norm_g"][i, 1]), sh2, sc2], [d], 256, lc)
        a = pmatmul(h2, p["mlp_w1"][i], "mlp1_%d" % i)
        (a,) = rowop(f_relu2, "relu2_%d" % i, [a], [], [a.shape[1]], 128)
        m = pmatmul(a, p["mlp_w2"][i], "mlp2_%d" % i)
        (xc,) = rowop(f_resid, "resid_b%d" % i, [xc, m], [g2], [d], 256, lc)
    (rows,) = rowop(f_loss, "loss", [xc, tgt], [_par(p["final_g"])], [LANES], 256)
    return 0.5 * jnp.sum(rows[:, 0])


def exchange(x, name):
    def body(x_ref, o_ref, send_sems, recv_sems, local_sem):
        mx, my, mc = lax.axis_index("x"), lax.axis_index("y"), lax.axis_index("c")
        me = 4 * mx + 2 * my + mc
        copies = []
        for k in range(1, N_DEV):
            px = 1 - mx if (k >> 2) & 1 else mx
            py = 1 - my if (k >> 1) & 1 else my
            pc = 1 - mc if k & 1 else mc
            cp = pltpu.make_async_remote_copy(src_ref=x_ref, dst_ref=o_ref.at[me], send_sem=send_sems.at[k - 1],
                                              recv_sem=recv_sems.at[k - 1], device_id=(px, py, pc), device_id_type=MESH)
            cp.start()
            copies.append(cp)
        mine = pltpu.make_async_copy(x_ref, o_ref.at[me], local_sem)
        mine.start()
        for cp in copies:
            cp.wait_recv()
        for cp in copies:
            cp.wait_send()
        mine.wait()

    return pl.pallas_call(
        body, out_shape=SDS((N_DEV,) + x.shape, x.dtype),
        in_specs=[pl.BlockSpec(memory_space=pl.ANY)], out_specs=pl.BlockSpec(memory_space=pl.ANY),
        scratch_shapes=[pltpu.SemaphoreType.DMA((N_DEV - 1,)), pltpu.SemaphoreType.DMA((N_DEV - 1,)),
                        pltpu.SemaphoreType.DMA],
        name=name)(x)


def _place():
    mx, my, mc = lax.axis_index("x"), lax.axis_index("y"), lax.axis_index("c")
    chips = [(1 - mx, my), (mx, 1 - my), (1 - mx, 1 - my)]
    return mx, my, mc, chips


def _any_specs(n):
    return [pl.BlockSpec(memory_space=pl.ANY)] * n


def gather_two_level(arrs, name):
    n = len(arrs)

    def body(*refs):
        x_refs, o_refs = refs[:n], refs[n:2 * n]
        send_sems, recv_sems, local_sems = refs[2 * n:]
        mx, my, mc, chips = _place()
        me, sib = (mx, my, mc), (mx, my, 1 - mc)

        def slot(px, py, pc):
            return 4 * px + 2 * py + pc

        def copy(a, k, block, to, src=None):
            dst = o_refs[a].at[slot(*block)]
            return pltpu.make_async_remote_copy(src_ref=dst if src is None else src, dst_ref=dst, send_sem=send_sems.at[a, k],
                                                recv_sem=recv_sems.at[a, k], device_id=to, device_id_type=MESH)

        mine = [pltpu.make_async_copy(x_refs[a], o_refs[a].at[slot(*me)], local_sems.at[a]) for a in range(n)]
        for cp in mine:
            cp.start()
        sent = []
        for a in range(n):
            sent += [copy(a, 1 + j, me, (*chip, mc), src=x_refs[a]) for j, chip in enumerate(chips)]
        for a in range(n):
            sent.append(copy(a, 0, me, sib, src=x_refs[a]))
        for cp in sent:
            cp.start()
        for a in range(n):
            for j, chip in enumerate(chips):
                copy(a, 1 + j, (*chip, mc), me).wait_recv()
                fwd = copy(a, 4 + j, (*chip, mc), sib)
                fwd.start()
                sent.append(fwd)
        for a in range(n):
            copy(a, 0, sib, me).wait_recv()
            for j, chip in enumerate(chips):
                copy(a, 4 + j, (*chip, 1 - mc), me).wait_recv()
        for cp in sent:
            cp.wait_send()
        for cp in mine:
            cp.wait()

    return pl.pallas_call(
        body, out_shape=[SDS((N_DEV,) + a.shape, a.dtype) for a in arrs], in_specs=_any_specs(n), out_specs=_any_specs(n),
        scratch_shapes=[pltpu.SemaphoreType.DMA((n, 7)), pltpu.SemaphoreType.DMA((n, 7)), pltpu.SemaphoreType.DMA((n,))],
        name=name)(*arrs)


def scatter_sibling(parts, name):
    n = len(parts)

    def body(*refs):
        x_refs, o_refs = refs[:n], refs[n:2 * n]
        send_sems, recv_sems = refs[2 * n:]
        mx, my, mc, _ = _place()
        copies = []
        for a in range(n):
            for b in range(4):
                cp = pltpu.make_async_remote_copy(src_ref=x_refs[a].at[b, 1 - mc], dst_ref=o_refs[a].at[b],
                                                  send_sem=send_sems.at[a, b], recv_sem=recv_sems.at[a, b],
                                                  device_id=(mx, my, 1 - mc), device_id_type=MESH)
                cp.start()
                copies.append(cp)
        for cp in copies:
            cp.wait_recv()
        for cp in copies:
            cp.wait_send()

    return pl.pallas_call(
        body, out_shape=[SDS((4,) + p.shape[2:], p.dtype) for p in parts], in_specs=_any_specs(n), out_specs=_any_specs(n),
        scratch_shapes=[pltpu.SemaphoreType.DMA((n, 4)), pltpu.SemaphoreType.DMA((n, 4))], name=name)(*parts)


def pair_sum(mine, theirs, name):
    _, rows, cols = mine.shape
    tr = _pick(rows, max(8, (1 << 19) // cols), 8)

    def body(a_ref, b_ref, o_ref):
        o_ref[...] = (a_ref[...].astype(F32) + b_ref[...].astype(F32)).astype(o_ref.dtype)

    spec = pl.BlockSpec((None, tr, cols), lambda b, i: (b, i, 0))
    return pl.pallas_call(body, grid=(4, rows // tr), in_specs=[spec, spec], out_specs=spec,
                          out_shape=SDS(mine.shape, mine.dtype), compiler_params=_cp("parallel", "parallel"), name=name)(mine, theirs)


def scatter_chips(sums, name):
    n = len(sums)

    def body(*refs):
        x_refs, o_refs = refs[:n], refs[n:2 * n]
        send_sems, recv_sems, local_sems = refs[2 * n:]
        mx, my, mc, chips = _place()
        my_chip = 2 * mx + my
        mine = [pltpu.make_async_copy(x_refs[a].at[my_chip], o_refs[a].at[my_chip], local_sems.at[a]) for a in range(n)]
        for cp in mine:
            cp.start()
        copies = []
        for a in range(n):
            for j, (px, py) in enumerate(chips):
                cp = pltpu.make_async_remote_copy(src_ref=x_refs[a].at[2 * px + py], dst_ref=o_refs[a].at[my_chip],
                                                  send_sem=send_sems.at[a, j], recv_sem=recv_sems.at[a, j],
                                                  device_id=(px, py, mc), device_id_type=MESH)
                cp.start()
                copies.append(cp)
        for cp in copies:
            cp.wait_recv()
        for cp in copies:
            cp.wait_send()
        for cp in mine:
            cp.wait()

    return pl.pallas_call(
        body, out_shape=[SDS(p.shape, p.dtype) for p in sums], in_specs=_any_specs(n), out_specs=_any_specs(n),
        scratch_shapes=[pltpu.SemaphoreType.DMA((n, 3)), pltpu.SemaphoreType.DMA((n, 3)), pltpu.SemaphoreType.DMA((n,))],
        name=name)(*sums)


def sum_parts(parts, name):
    n_p, rows, _ = parts.shape
    tr = _pick(rows, 1024, 8)

    def body(p_ref, o_ref):
        acc = p_ref[0].astype(F32)
        for j in range(1, n_p):
            acc = acc + p_ref[j].astype(F32)
        o_ref[...] = acc

    return pl.pallas_call(body, grid=(rows // tr,), in_specs=[pl.BlockSpec((n_p, tr, LANES), lambda i: (0, i, 0))],
                          out_specs=pl.BlockSpec((tr, LANES), lambda i: (i, 0)), out_shape=SDS((rows, LANES), F32),
                          compiler_params=_cp("parallel"), name=name)(parts)


def adamw(w, parts, m, v, name):
    rows, cols = w.shape
    n_p = parts.shape[0]
    tr = _pick(rows, max(8, (1 << 18) // cols), 8)
    c1 = 1.0 - ADAM_B1 ** ADAM_STEP
    c2 = 1.0 - ADAM_B2 ** ADAM_STEP

    def body(w_ref, p_ref, m_ref, v_ref, g_out, d_out, m_out, v_out):
        g = p_ref[0].astype(F32)
        for j in range(1, n_p):
            g = g + p_ref[j].astype(F32)
        m_new = ADAM_B1 * m_ref[...] + (1.0 - ADAM_B1) * g
        v_new = ADAM_B2 * v_ref[...] + (1.0 - ADAM_B2) * jnp.square(g)
        g_out[...] = g
        m_out[...] = m_new
        v_out[...] = v_new
        d_out[...] = -ADAM_LR * ((m_new / c1) / (jnp.sqrt(v_new / c2) + ADAM_EPS) + ADAM_WD * w_ref[...])

    spec = pl.BlockSpec((tr, cols), lambda i: (i, 0))
    return pl.pallas_call(
        body, grid=(rows // tr,), in_specs=[spec, pl.BlockSpec((n_p, tr, cols), lambda i: (0, i, 0)), spec, spec],
        out_specs=[spec] * 4, out_shape=[SDS((rows, cols), F32)] * 4, compiler_params=_cp("parallel"), name=name)(w, parts, m, v)


FWD_NAMES = ["x", "c", "ctx", "c_ctx", "ada_w", "ada_b", "norm_g", "mlp_w1", "mlp_w2", "final_g", "ssd_w_in", "ssd_conv_w",
             "ssd_conv_b", "ssd_dt_bias", "ssd_a_log", "ssd_d", "ssd_norm_g", "ssd_w_out", "ret_w_in", "ret_log_decay",
             "ret_w_out", "hgrn_w_in", "hgrn_lb_logits", "hgrn_norm_g", "hgrn_w_out", "gdn_w_in", "gdn_conv_w", "gdn_dt_bias",
             "gdn_a_log", "gdn_norm_g", "gdn_w_out"]
WEIGHTS = FWD_NAMES[3:]
BIG = {"mlp_w1": 2, "mlp_w2": 1, "ssd_w_in": 2, "ssd_w_out": 1, "ret_w_in": 2, "ret_w_out": 1, "hgrn_w_in": 2,
       "hgrn_w_out": 1, "gdn_w_in": 2, "gdn_w_out": 1}
SMALL_SHARDED = ["norm_g", "ssd_conv_w", "gdn_conv_w", "hgrn_norm_g"]
SMALL = [n for n in WEIGHTS if n not in BIG and n != "ada_w"]


def _to_full(g8, axis):
    _, l, a, b = g8.shape
    if axis == 1:
        return g8.transpose(1, 0, 2, 3).reshape(l, N_DEV * a, b)
    return g8.transpose(1, 2, 0, 3).reshape(l, a, N_DEV * b)


def _to_shards(full, axis):
    l, a, b = full.shape
    if axis == 1:
        return full.reshape(l, N_DEV, a // N_DEV, b).transpose(1, 0, 2, 3)
    return full.reshape(l, a, N_DEV, b // N_DEV).transpose(2, 0, 1, 3)


def _pack_small(arrs):
    parts, meta, off = [], [], 0
    for a in arrs:
        n = a.size
        npad = -(-n // LANES) * LANES
        parts.append(jnp.pad(a.reshape(-1).astype(F32), (0, npad - n)))
        meta.append((off, n, a.shape))
        off += npad
    rows = -(-(off // LANES) // 8) * 8
    flat = jnp.concatenate(parts)
    flat = jnp.pad(flat, (0, rows * LANES - off))
    return flat.reshape(rows, LANES), meta


def _unpack_small(buf, meta):
    flat = buf.reshape(buf.shape[:-2] + (-1,))
    return [flat[..., off:off + n].reshape(buf.shape[:-2] + tuple(shape)) for off, n, shape in meta]


def _my_shard(full, me, n_local):
    return lax.dynamic_slice_in_dim(full, me * n_local, n_local, axis=full.ndim - 1)


def kernel(*args):
    n_f = len(FWD_NAMES)
    inp = dict(zip(FWD_NAMES, args[:n_f]))
    tgt = args[n_f][0]
    n_w = len(WEIGHTS)
    mom_m = dict(zip(WEIGHTS, args[n_f + 1:n_f + 1 + n_w]))
    mom_v = dict(zip(WEIGHTS, args[n_f + 1 + n_w:n_f + 1 + 2 * n_w]))
    x, ctx, c = inp["x"][0], inp["ctx"][0], inp["c"]
    d = x.shape[1]
    me = 4 * lax.axis_index("x") + 2 * lax.axis_index("y") + lax.axis_index("c")

    buf, meta = _pack_small([c] + [inp[n] for n in SMALL_SHARDED])
    got = _unpack_small(exchange(buf, "gather_small"), meta)
    c_all = got[0].reshape(N_DEV, d)
    small_full = {}
    for n, g8 in zip(SMALL_SHARDED, got[1:]):
        small_full[n] = jnp.moveaxis(g8, 0, -2).reshape(g8.shape[1:-1] + (N_DEV * g8.shape[-1],))
    cond_in = jnp.concatenate([c_all, inp["c_ctx"][None]], axis=0)
    cond = _silu(cond_in)
    cond16 = jnp.pad(cond, ((0, 16 - cond.shape[0]), (0, 0)))

    ada_w = inp["ada_w"]
    n_ada = ada_w.shape[2]
    mod_loc = jnp.stack([_mm(cond16, ada_w[i], "nn", "ada_fwd")[:N_DEV + 1] for i in range(DEPTH)])
    mbuf, mmeta = _pack_small([mod_loc])
    (mod8,) = _unpack_small(exchange(mbuf, "gather_mod"), mmeta)
    mod_full = mod8.transpose(1, 2, 0, 3).reshape(DEPTH, N_DEV + 1, N_DEV * n_ada) + inp["ada_b"][:, None, :]
    mod2 = jnp.stack([mod_full[:, N_DEV], lax.dynamic_index_in_dim(mod_full, me, axis=1, keepdims=False)], axis=1)

    big = list(BIG)
    wg = gather_two_level([inp[n].astype(BF16).reshape(-1, inp[n].shape[2]) for n in big], "gather_weights")
    p = {}
    for n, g8 in zip(big, wg):
        p[n] = _to_full(g8.reshape((N_DEV,) + inp[n].shape), BIG[n])
    p["gdn_w_in_main"], p["gdn_w_in_gate"] = p["gdn_w_in"][0, :, :6 * d], p["gdn_w_in"][0, :, 6 * d:]
    for n in ("ssd_w_in", "ssd_w_out", "ret_w_in", "ret_w_out", "hgrn_w_in", "hgrn_w_out", "gdn_w_out"):
        p[n] = p[n][0]
    del p["gdn_w_in"]
    for n in SMALL:
        if n not in ("c_ctx", "ada_b"):
            p[n] = small_full[n] if n in small_full else inp[n]
    for n in ("ssd_conv_w", "ssd_conv_b", "ssd_dt_bias", "ssd_a_log", "ssd_d", "ssd_norm_g", "ret_log_decay", "hgrn_norm_g",
              "gdn_conv_w", "gdn_dt_bias", "gdn_a_log", "gdn_norm_g"):
        p[n] = p[n][0]

    loss_loc, (g_mod2, g_x, g_p) = jax.value_and_grad(
        lambda mod2_, x_, p_: local_loss(x_, ctx, mod2_, tgt, p_), argnums=(0, 1, 2))(mod2, x, p)

    small_g_names = [n for n in SMALL if n not in ("c_ctx", "ada_b")]
    gbuf, gmeta = _pack_small([loss_loc.reshape(1), g_mod2] + [g_p[n] for n in small_g_names])
    g8 = exchange(gbuf, "gather_small_grads")
    gsum = _unpack_small(sum_parts(g8, "sum_small_grads"), gmeta)
    loss = gsum[0][0]
    g_small = dict(zip(small_g_names, gsum[2:]))
    for n in ("ssd_conv_w", "ssd_conv_b", "ssd_dt_bias", "ssd_a_log", "ssd_d", "ssd_norm_g", "ret_log_decay", "hgrn_norm_g",
              "gdn_conv_w", "gdn_dt_bias", "gdn_a_log", "gdn_norm_g"):
        g_small[n] = g_small[n][None]
    dmod_each = _unpack_small(g8, gmeta)[1]
    dmod9 = jnp.concatenate([dmod_each[:, :, 1].transpose(1, 0, 2), gsum[1][:, 0:1]], axis=1)
    g_small["ada_b"] = gsum[1][:, 0] + gsum[1][:, 1]
    dmod16 = jnp.pad(_my_shard(dmod9, me, n_ada), ((0, 0), (0, 16 - dmod9.shape[1]), (0, 0)))
    g_ada_w = jnp.stack([_mm(cond16, dmod16[i], "tn", "ada_dw") for i in range(DEPTH)])
    dcond_part = _mm(dmod16[0], ada_w[0], "nt", "ada_dx")
    for i in range(1, DEPTH):
        dcond_part = dcond_part + _mm(dmod16[i], ada_w[i], "nt", "ada_dx")
    cbuf, cmeta = _pack_small([dcond_part[N_DEV]])
    (dcond8,) = _unpack_small(sum_parts(exchange(cbuf, "gather_dcond"), "sum_dcond"), cmeta)
    g_small["c_ctx"] = jax.vjp(_silu, inp["c_ctx"])[1](dcond8)[0]

    g_p["gdn_w_in"] = jnp.concatenate([g_p.pop("gdn_w_in_main"), g_p.pop("gdn_w_in_gate")], axis=1)
    my_core = lax.axis_index("c")
    blocks = []
    for n in big:
        gf = g_p[n] if g_p[n].ndim == 3 else g_p[n][None]
        blocks.append(_to_shards(gf, BIG[n]).reshape(4, 2, -1, inp[n].shape[2]))
    from_sibling = scatter_sibling(blocks, "scatter_grads_sibling")
    sums = [pair_sum(lax.dynamic_index_in_dim(b, my_core, axis=1, keepdims=False), t, "pair_sum_" + n)
            for n, b, t in zip(big, blocks, from_sibling)]
    parts = scatter_chips(sums, "scatter_grads_chips")

    out = {}

    def view(a):
        return a.reshape(-1, a.shape[-1])

    for n, pr in zip(big, parts):
        res = adamw(view(inp[n]), pr, view(mom_m[n]), view(mom_v[n]), "adamw_" + n)
        out[n] = [r.reshape(inp[n].shape) for r in res]
    res = adamw(view(ada_w), view(g_ada_w)[None], view(mom_m["ada_w"]), view(mom_v["ada_w"]), "adamw_ada_w")
    out["ada_w"] = [r.reshape(ada_w.shape) for r in res]
    for n in SMALL_SHARDED:
        g_small[n] = _my_shard(g_small[n], me, inp[n].shape[-1])
    wb, wmeta = _pack_small([inp[n] for n in SMALL])
    gb, _ = _pack_small([g_small[n] for n in SMALL])
    mb, _ = _pack_small([mom_m[n] for n in SMALL])
    vb, _ = _pack_small([mom_v[n] for n in SMALL])
    res = [_unpack_small(r, wmeta) for r in adamw(wb, gb[None], mb, vb, "adamw_small")]
    for j, n in enumerate(SMALL):
        out[n] = [r[j] for r in res]

    outs = [loss, g_x[None]]
    for k in range(4):
        outs += [out[n][k] for n in WEIGHTS]
    return tuple(outs)
```

```python
import functools
import math

import jax
import jax.numpy as jnp
from jax import lax
from jax.experimental import pallas as pl
from jax.experimental.pallas import tpu as pltpu

F32 = jnp.float32
BF16 = jnp.bfloat16
HI = lax.Precision.HIGHEST

N_DEV = 8
CHUNK = 64
SUB = 16
GROUPS_PER_STEP = 2
GRID_W = 64
ROPE_BASE = 10000.0
NORM_EPS = 1e-6
DEPTH = 4
SSD_GROUPS = 8
SSD_STATE = 128
SSD_HEAD_DIM = 64
RET_HEADS = 8
HEAD128 = 128
ADAM_LR, ADAM_B1, ADAM_B2, ADAM_EPS, ADAM_WD, ADAM_STEP = 0.001, 0.9, 0.999, 1e-08, 0.01, 10

V7X_VMEM_BYTES = 64 * 1024 * 1024
VMEM_LIMIT = (V7X_VMEM_BYTES * 3) // 4
ROW_PIPELINE_BYTES = V7X_VMEM_BYTES // 4
LANES = 128
MESH = pl.DeviceIdType.MESH

SDS = jax.ShapeDtypeStruct


def _cp(*sem):
    return pltpu.CompilerParams(dimension_semantics=tuple(sem), vmem_limit_bytes=VMEM_LIMIT)


def _pick(n, cap, quantum=LANES):
    best = None
    d = quantum
    while d <= min(n, cap):
        if n % d == 0:
            best = d
        d += quantum
    return n if best is None else best


def _mm(a, b, mode, name, out_dtype=F32, carry=()):
    if mode == "nn":
        (m, k), n = a.shape, b.shape[1]
    elif mode == "nt":
        (m, k), n = a.shape, b.shape[0]
    else:
        (k, m), n = a.shape, b.shape[1]
    tm, tn, tk = _pick(m, 1024), _pick(n, 1280), _pick(k, 2048)
    nk = k // tk
    if mode == "nn":
        a_spec = pl.BlockSpec((tm, tk), lambda i, j, kk: (i, kk))
        b_spec = pl.BlockSpec((tk, tn), lambda i, j, kk: (kk, j))
        dims = (((1,), (0,)), ((), ()))
    elif mode == "nt":
        a_spec = pl.BlockSpec((tm, tk), lambda i, j, kk: (i, kk))
        b_spec = pl.BlockSpec((tn, tk), lambda i, j, kk: (j, kk))
        dims = (((1,), (1,)), ((), ()))
    else:
        a_spec = pl.BlockSpec((tk, tm), lambda i, j, kk: (kk, i))
        b_spec = pl.BlockSpec((tk, tn), lambda i, j, kk: (kk, j))
        dims = (((0,), (0,)), ((), ()))

    n_x = len(carry)
    ni, nj = m // tm, n // tn

    def body(*refs):
        a_ref, b_ref = refs[:2]
        x_refs = refs[2:2 + n_x]
        o_ref = refs[2 + n_x]
        g_refs = refs[3 + n_x:3 + 2 * n_x]
        acc_ref = refs[3 + 2 * n_x]
        sems = refs[4 + 2 * n_x:]
        i, j, kk = pl.program_id(0), pl.program_id(1), pl.program_id(2)

        if n_x:
            @pl.when((i == 0) & (j == 0) & (kk == 0))
            def _():
                _gather_start(x_refs, g_refs, *sems)

        @pl.when(kk == 0)
        def _():
            acc_ref[...] = jnp.zeros_like(acc_ref)

        acc_ref[...] += lax.dot_general(a_ref[...].astype(BF16), b_ref[...].astype(BF16), dims,
                                        preferred_element_type=F32)

        @pl.when(kk == nk - 1)
        def _():
            o_ref[...] = acc_ref[...].astype(o_ref.dtype)

        if n_x:
            @pl.when((i == ni - 1) & (j == nj - 1) & (kk == nk - 1))
            def _():
                _gather_finish(x_refs, g_refs, *sems)

    res = pl.pallas_call(
        body, grid=(ni, nj, nk), in_specs=[a_spec, b_spec] + _any_specs(n_x),
        out_specs=[pl.BlockSpec((tm, tn), lambda i, j, kk: (i, j))] + _any_specs(n_x),
        out_shape=[SDS((m, n), out_dtype)] + [SDS((N_DEV,) + x.shape, x.dtype) for x in carry],
        scratch_shapes=[pltpu.VMEM((tm, tn), F32)] + (_gather_sems(n_x) if n_x else []),
        compiler_params=_cp("arbitrary", "arbitrary", "arbitrary") if n_x else _cp("parallel", "parallel", "arbitrary"),
        name=name)(a, b, *carry)
    return (res[0], list(res[1:])) if n_x else res[0]


def pmatmul(a, w, name, carry=None):
    if carry:
        @jax.custom_vjp
        def cop(a, w, *xs):
            y, got = _mm(a, w, "nn", name + "_fwd", carry=xs)
            return y, tuple(got)

        def cfwd(a, w, *xs):
            return cop(a, w, *xs), (a, w, xs)

        def cbwd(res, cts):
            a, w, xs = res
            return (_mm(cts[0], w, "nt", name + "_dx"), _mm(a, cts[0], "tn", name + "_dw", out_dtype=w.dtype),
                    *[jnp.zeros_like(x) for x in xs])

        cop.defvjp(cfwd, cbwd)
        y, got = cop(a, w, *carry)
        return y, list(got)

    @jax.custom_vjp
    def op(a, w):
        return _mm(a, w, "nn", name + "_fwd")

    def fwd(a, w):
        return _mm(a, w, "nn", name + "_fwd"), (a, w)

    def bwd(res, g):
        a, w = res
        return _mm(g, w, "nt", name + "_dx"), _mm(a, g, "tn", name + "_dw", out_dtype=w.dtype)

    op.defvjp(fwd, bwd)
    return op(a, w)


def rowop(f, name, rows, pars, out_ws, tile, ctx_rows=0):
    t = rows[0].shape[0]
    tile = math.gcd(math.gcd(tile, t), ctx_rows if ctx_rows else t)
    row_bytes = 2 * 4 * (2 * sum(r.shape[1] for r in rows) + sum(out_ws))
    while tile > 8 and tile % 2 == 0 and tile * row_bytes > ROW_PIPELINE_BYTES:
        tile //= 2
    nt = t // tile
    cut = ctx_rows // tile
    n_r, n_p, n_o = len(rows), len(pars), len(out_ws)
    segs = [p.shape[0] for p in pars]
    assert all(s in (1, 2) for s in segs)

    def par_map(s):
        if s == 1:
            return lambda i: (0, 0, 0)
        return lambda i: (jnp.where(i >= cut, 1, 0), 0, 0)

    row_specs = [pl.BlockSpec((tile, r.shape[1]), lambda i: (i, 0)) for r in rows]
    par_specs = [pl.BlockSpec((None, 1, p.shape[2]), par_map(p.shape[0])) for p in pars]
    out_specs = [pl.BlockSpec((tile, w), lambda i: (i, 0)) for w in out_ws]
    out_shape = [SDS((t, w), F32) for w in out_ws]

    def fwd_call(*args):
        def body(*refs):
            outs = f(*[r[...] for r in refs[:n_r + n_p]])
            for o_ref, o in zip(refs[n_r + n_p:], outs):
                o_ref[...] = o

        return tuple(pl.pallas_call(body, grid=(nt,), in_specs=row_specs + par_specs, out_specs=out_specs,
                                    out_shape=out_shape, compiler_params=_cp("parallel"), name=name + "_fwd")(*args))

    def bwd_call(args, gouts):
        def body(*refs):
            ins = [r[...] for r in refs[:n_r + n_p]]
            gs = tuple(r[...] for r in refs[n_r + n_p:n_r + n_p + n_o])
            d_refs = refs[n_r + n_p + n_o:]
            _, vjp = jax.vjp(f, *ins)
            grads = vjp(gs)
            for ref, g in zip(d_refs[:n_r], grads[:n_r]):
                ref[...] = g
            i = pl.program_id(0)
            for ref, g, s in zip(d_refs[n_r:], grads[n_r:], segs):
                first = (i == 0) if s == 1 else jnp.logical_or(i == 0, i == cut)

                @pl.when(first)
                def _(ref=ref, g=g):
                    ref[...] = g

                @pl.when(jnp.logical_not(first))
                def _(ref=ref, g=g):
                    ref[...] += g

        d_specs = row_specs + par_specs
        d_shape = [SDS(r.shape, F32) for r in rows] + [SDS(p.shape, F32) for p in pars]
        return tuple(pl.pallas_call(body, grid=(nt,), in_specs=row_specs + par_specs + out_specs, out_specs=d_specs,
                                    out_shape=d_shape, compiler_params=_cp("arbitrary"), name=name + "_bwd")(*args, *gouts))

    @jax.custom_vjp
    def op(*args):
        return fwd_call(*args)

    op.defvjp(lambda *args: (fwd_call(*args), args), bwd_call)
    return op(*rows, *pars)


def scanop(chunk_fn, name, cols, col_ws, grps, state_shape, y_w, rev=False, ctx_rows=0, unroll=1, lockstep=False):
    t = cols[0].shape[0]
    g_n = cols[0].shape[1] // col_ws[0]
    u_n = unroll if g_n % unroll == 0 else 1
    gs_n = g_n // u_n
    nc = t // CHUNK
    ncx = ctx_rows // CHUNK
    n_c, n_g = len(cols), len(grps)

    def order(c):
        if not rev:
            return c
        return jnp.where(c < ncx, ncx - 1 - c, nc - 1 - (c - ncx))

    def specs(cmap):
        col_specs = [pl.BlockSpec((CHUNK, u_n * w), lambda g, c: (cmap(c), g)) for w in col_ws]
        grp_specs = [pl.BlockSpec((u_n, CHUNK, a.shape[2]), lambda g, c: (g, cmap(c), 0)) for a in grps]
        y_spec = pl.BlockSpec((CHUNK, u_n * y_w), lambda g, c: (cmap(c), g))
        s_spec = pl.BlockSpec((u_n, None) + state_shape, lambda g, c: (g, cmap(c), 0, 0))
        return col_specs, grp_specs, y_spec, s_spec

    def group_ins(ins, u):
        return ([a[:, u * w:(u + 1) * w] for a, w in zip(ins[:n_c], col_ws)] + [a[u] for a in ins[n_c:]])

    if lockstep:
        step_fn = chunk_fn
    else:
        def step_fn(s_list, ins_list):
            return [chunk_fn(s, *ins) for s, ins in zip(s_list, ins_list)]

    def fwd_call(*args):
        col_specs, grp_specs, y_spec, s_spec = specs(order)

        def body(*refs):
            ins = [r[...] for r in refs[:n_c + n_g]]
            y_ref, sall_ref, s_ref = refs[n_c + n_g:]

            @pl.when(pl.program_id(1) == 0)
            def _():
                s_ref[...] = jnp.zeros_like(s_ref)

            s = s_ref[...]
            sall_ref[...] = s
            res = step_fn([s[u] for u in range(u_n)], [group_ins(ins, u) for u in range(u_n)])
            y_ref[...] = jnp.concatenate([y for _, y in res], axis=1)
            s_ref[...] = jnp.stack([s_new for s_new, _ in res])

        return pl.pallas_call(
            body, grid=(gs_n, nc), in_specs=col_specs + grp_specs, out_specs=[y_spec, s_spec],
            out_shape=[SDS((t, g_n * y_w), F32), SDS((g_n, nc) + state_shape, F32)],
            scratch_shapes=[pltpu.VMEM((u_n,) + state_shape, F32)],
            compiler_params=_cp("parallel", "arbitrary"), name=name + "_fwd")(*args)

    def bwd_call(res, gy):
        args, s_all = res
        col_specs, grp_specs, y_spec, s_spec = specs(lambda c: order(nc - 1 - c))

        def body(*refs):
            ins = [r[...] for r in refs[:n_c + n_g]]
            s_prev = refs[n_c + n_g][...]
            dy = refs[n_c + n_g + 1][...]
            d_refs = refs[n_c + n_g + 2:-1]
            ds_ref = refs[-1]

            @pl.when(pl.program_id(1) == 0)
            def _():
                ds_ref[...] = jnp.zeros_like(ds_ref)

            ds = ds_ref[...]
            _, vjp = jax.vjp(step_fn, [s_prev[u] for u in range(u_n)], [group_ins(ins, u) for u in range(u_n)])
            d_s, d_ins = vjp([(ds[u], dy[:, u * y_w:(u + 1) * y_w]) for u in range(u_n)])
            grads = [[d_s[u]] + list(d_ins[u]) for u in range(u_n)]
            ds_ref[...] = jnp.stack([g[0] for g in grads])
            for j, ref in enumerate(d_refs):
                ref[...] = (jnp.concatenate([g[1 + j] for g in grads], axis=1) if j < n_c
                            else jnp.stack([g[1 + j] for g in grads]))

        return tuple(pl.pallas_call(
            body, grid=(gs_n, nc), in_specs=col_specs + grp_specs + [s_spec, y_spec], out_specs=col_specs + grp_specs,
            out_shape=[SDS(a.shape, F32) for a in args],
            scratch_shapes=[pltpu.VMEM((u_n,) + state_shape, F32)],
            compiler_params=_cp("parallel", "arbitrary"), name=name + "_bwd")(*args, s_all, gy))

    @jax.custom_vjp
    def op(*args):
        return fwd_call(*args)[0]

    def fwd(*args):
        y, s_all = fwd_call(*args)
        return y, (args, s_all)

    op.defvjp(fwd, bwd_call)
    return op(*cols, *grps)


def _dg(a, b, ca, cb, prec=None):
    return lax.dot_general(a, b, (((ca,), (cb,)), ((), ())), precision=prec, preferred_element_type=F32)


@functools.partial(jax.custom_vjp, nondiff_argnums=(2, 3))
def mmb(a, b, ca, cb):
    return _dg(a.astype(BF16), b.astype(BF16), ca, cb)


def _mmb_fwd(a, b, ca, cb):
    return mmb(a, b, ca, cb), (a, b)


def _mmb_bwd(ca, cb, res, g):
    a, b = res
    ab, bb, gb = a.astype(BF16), b.astype(BF16), g.astype(BF16)
    da = _dg(gb, bb, 1, 1 - cb) if ca == 1 else _dg(bb, gb, 1 - cb, 1)
    db = _dg(ab, gb, 1 - ca, 0) if cb == 0 else _dg(gb, ab, 0, 1 - ca)
    return da, db


mmb.defvjp(_mmb_fwd, _mmb_bwd)


def mmh(a, b, ca=1, cb=0):
    return _dg(a, b, ca, cb, HI)


def _masks(n, rev=False):
    i = lax.broadcasted_iota(jnp.int32, (n, n), 0)
    j = lax.broadcasted_iota(jnp.int32, (n, n), 1)
    return (i <= j, i < j) if rev else (i >= j, i > j)


def _silu(x):
    return x * jax.nn.sigmoid(x)


def _softplus(x):
    return jnp.maximum(x, 0.0) + jnp.log(1.0 + jnp.exp(-jnp.abs(x)))


def _expand_heads(p, width):
    h = p.shape[1]
    lane = lax.broadcasted_iota(jnp.int32, (h, h * width), 1)
    row = lax.broadcasted_iota(jnp.int32, (h, h * width), 0)
    e = jnp.where(lane // width == row, 1.0, 0.0).astype(F32)
    return mmh(jnp.broadcast_to(p, (8, h)), e)[0:1]


def _group_apply(x, width, fn):
    n = x.shape[1] // width
    return jnp.concatenate([fn(x[:, g * width:(g + 1) * width]) for g in range(n)], axis=1)


def _rms(x):
    return x * lax.rsqrt(jnp.mean(x * x, axis=-1, keepdims=True) + NORM_EPS)


@jax.custom_vjp
def unit_lower_inv(ls):
    n = ls[0].shape[0]
    eye = jnp.where(_masks(n)[0] & jnp.logical_not(_masks(n)[1]), 1.0, 0.0).astype(F32)
    ps = [-l for l in ls]
    xs = [eye + p for p in ps]
    for _ in range(int(math.log2(n)) - 1):
        ps = [mmh(p, p) for p in ps]
        xs = [x + mmh(x, p) for x, p in zip(xs, ps)]
    return tuple(xs)


def _uli_fwd(ls):
    xs = unit_lower_inv(ls)
    return xs, xs


def _uli_bwd(xs, gs):
    ts = [_dg(x, g, 0, 0, HI) for x, g in zip(xs, gs)]
    return (tuple(-_dg(t, x, 1, 1, HI) for t, x in zip(ts, xs)),)


unit_lower_inv.defvjp(_uli_fwd, _uli_bwd)


def scalar_decay_chunk(r_n, n, p, with_dt, rev):
    def fn(s, q, k, v, la, *rest):
        c = q.shape[0]
        incl, _ = _masks(c, rev)
        tril = incl.astype(F32)
        cum = mmh(tril, la)
        cum_t = _dg(la, tril, 0, 1, HI)
        tot = cum[0:1, :] if rev else cum[c - 1:c, :]
        scores = mmb(q, k, 1, 1)
        ys, ss = [], []
        for r in range(r_n):
            cr = cum[:, r:r + 1]
            dec = jnp.exp(jnp.where(incl, cr - cum_t[r:r + 1, :], -jnp.inf))
            vr = v[:, r * p:(r + 1) * p]
            if with_dt:
                vr = vr * rest[0][:, r:r + 1]
            sr = s[r * n:(r + 1) * n, :]
            ys.append(mmb(scores * dec, vr, 1, 0) + mmb(q, sr, 1, 0) * jnp.exp(cr))
            ss.append(jnp.exp(tot[:, r:r + 1]) * sr + mmb(k, jnp.exp(tot[:, r:r + 1] - cr) * vr, 0, 0))
        return jnp.concatenate(ss, axis=0), jnp.concatenate(ys, axis=1)

    return fn


def vector_decay_chunk(rev):
    return functools.partial(_vector_decay_chunk, rev)


def _vector_decay_chunk(rev, s, q, k, v, lf):
    c, kd = q.shape
    incl, _ = _masks(c, rev)
    cum = mmh(incl.astype(F32), lf)
    tot = cum[0:1, :] if rev else cum[c - 1:c, :]
    tot_col = _dg(lf, jnp.ones((c, 1), F32), 0, 0, HI)
    y_inter = mmb(q * jnp.exp(cum), s, 1, 0)
    s_new = jnp.exp(tot_col) * s + mmb(k * jnp.exp(tot - cum), v, 0, 0)
    i3 = lax.broadcasted_iota(jnp.int32, (SUB, SUB, 1), 0)
    j3 = lax.broadcasted_iota(jnp.int32, (SUB, SUB, 1), 1)
    ys = []
    for b in range(c // SUB):
        lo, hi = b * SUB, (b + 1) * SUB
        qb, kb, cb, vb = q[lo:hi], k[lo:hi], cum[lo:hi], v[lo:hi]
        dec = jnp.exp(jnp.where((i3 <= j3) if rev else (i3 >= j3), cb[:, None, :] - cb[None, :, :], -jnp.inf))
        a_diag = jnp.sum(qb[:, None, :] * kb[None, :, :] * dec, axis=-1)
        yb = mmb(a_diag, vb, 1, 0)
        if rev and hi < c:
            base = cum[hi:hi + 1, :]
            a_off = mmb(qb * jnp.exp(cb - base), k[hi:] * jnp.exp(base - cum[hi:]), 1, 1)
            yb = yb + mmb(a_off, v[hi:], 1, 0)
        if not rev and b > 0:
            base = cum[lo - 1:lo, :]
            a_off = mmb(qb * jnp.exp(cb - base), k[:lo] * jnp.exp(base - cum[:lo]), 1, 1)
            yb = yb + mmb(a_off, v[:lo], 1, 0)
        ys.append(yb)
    return s_new, y_inter + jnp.concatenate(ys, axis=0)


def delta_chunk(r_n, kd, vd, rev):
    def fn(s_list, ins_list):
        c = ins_list[0][0].shape[0]
        incl, strict = _masks(c, rev)
        tril = incl.astype(F32)
        heads = [(g, r) for g in range(len(ins_list)) for r in range(r_n)]
        q = [ins[0] for ins in ins_list]
        k = [ins[1] for ins in ins_list]
        cum = [mmh(tril, ins[4]) for ins in ins_list]
        cum_t = [_dg(ins[4], tril, 0, 1, HI) for ins in ins_list]
        kk = [mmb(a, a, 1, 1) for a in k]
        qk = [mmb(a, b, 1, 1) for a, b in zip(q, k)]
        cr = [cum[g][:, r:r + 1] for g, r in heads]
        br = [ins_list[g][3][:, r:r + 1] for g, r in heads]
        seg = [c_ - cum_t[g][r:r + 1, :] for c_, (g, r) in zip(cr, heads)]
        tot = [(cum[g][0:1, :] if rev else cum[g][c - 1:c, :])[:, r:r + 1] for g, r in heads]
        vr = [ins_list[g][2][:, r * vd:(r + 1) * vd] for g, r in heads]
        sr = [s_list[g][r * kd:(r + 1) * kd, :] for g, r in heads]
        xs = unit_lower_inv(tuple(b * kk[g] * jnp.exp(jnp.where(strict, sg, -jnp.inf))
                                  for b, sg, (g, r) in zip(br, seg, heads)))
        u = [mmh(x, v_ * b) for x, v_, b in zip(xs, vr, br)]
        w = [mmh(x, k[g] * (b * jnp.exp(c_))) for x, b, c_, (g, r) in zip(xs, br, cr, heads)]
        ws = [mmb(w_, s_, 1, 0) for w_, s_ in zip(w, sr)]
        v_new = [u_ - a for u_, a in zip(u, ws)]
        ya = [mmb(qk[g] * jnp.exp(jnp.where(incl, sg, -jnp.inf)), vn, 1, 0) for sg, vn, (g, r) in zip(seg, v_new, heads)]
        yb = [mmb(q[g], s_, 1, 0) * jnp.exp(c_) for s_, c_, (g, r) in zip(sr, cr, heads)]
        sn = [jnp.exp(t_) * s_ + mmb(k[g], jnp.exp(t_ - c_) * vn, 0, 0)
              for t_, s_, c_, vn, (g, r) in zip(tot, sr, cr, v_new, heads)]
        out = []
        for g in range(len(ins_list)):
            idx = [i for i, (g2, _) in enumerate(heads) if g2 == g]
            out.append((jnp.concatenate([sn[i] for i in idx], axis=0), jnp.concatenate([ya[i] + yb[i] for i in idx], axis=1)))
        return out

    return fn


def f_adaln(x, g, sh, sc):
    return ((_rms(x) * g) * (1.0 + sc) + sh,)


def f_resid(x, y, gate):
    return (x + gate * y,)


def f_relu2(h):
    return (jnp.square(jnp.maximum(h, 0.0)),)


def f_conv_bias(up, u, un, w0, w1, w2, b):
    return (_silu(w0 * up + w1 * u + w2 * un + b),)


def f_conv(up, u, un, w0, w1, w2):
    return (_silu(w0 * up + w1 * u + w2 * un),)


def f_dt(raw, bias, a_log):
    dt = _softplus(raw + bias)
    return dt, -jnp.exp(a_log) * dt


def f_ssd_post(yf, yb, xs, z, d_skip, ng):
    y = (yf + yb + _expand_heads(d_skip, SSD_HEAD_DIM) * xs) * _silu(z)
    return (_group_apply(y, y.shape[1] // SSD_GROUPS, _rms) * ng,)


def f_rope(scale):
    def f(q, qsw, cos, sin):
        return ((q * cos + qsw * sin) * scale,)
    return f


def f_ret_post(yf, yb, g):
    def ln(y):
        mu = jnp.mean(y, axis=-1, keepdims=True)
        var = jnp.mean(jnp.square(y - mu), axis=-1, keepdims=True)
        return (y - mu) * lax.rsqrt(var + NORM_EPS)
    y = yf + yb
    return (_group_apply(y, y.shape[1] // RET_HEADS, ln) * _silu(g),)


def f_lower_bound(layer):
    def f(logits):
        m = jnp.max(logits, axis=0, keepdims=True)
        e = jnp.exp(logits - m)
        p = e / jnp.sum(e, axis=0, keepdims=True)
        lb = p[1:2]
        for j in range(2, layer + 1):
            lb = lb + p[j:j + 1]
        return (jnp.broadcast_to(lb, logits.shape),)
    return f


def f_hgrn_gates(ff, fb, lb):
    def gates(f):
        log_sig = jnp.minimum(f, 0.0) - jnp.log(1.0 + jnp.exp(-jnp.abs(f)))
        a, b = jnp.log(lb), jnp.log(1.0 - lb) + log_sig
        m = jnp.maximum(a, b)
        return m + jnp.log(jnp.exp(a - m) + jnp.exp(b - m)), (1.0 - lb) * jax.nn.sigmoid(-f)
    lf_f, k_f = gates(ff)
    lf_b, k_b = gates(fb)
    return lf_f, k_f, lf_b, k_b


def f_hgrn_post(yf, yb, g, ng):
    return (_group_apply(yf + yb, HEAD128, _rms) * ng * _silu(g),)


def f_gdn_qk(q, k):
    def l2(x):
        return x * lax.rsqrt(jnp.sum(x * x, axis=-1, keepdims=True) + 1e-6)
    return _group_apply(q, HEAD128, l2) * HEAD128 ** -0.5, _group_apply(k, HEAD128, l2)


def f_gdn_gates(bt, a, dt_bias, a_log):
    return jax.nn.sigmoid(bt), -jnp.exp(a_log) * _softplus(a + dt_bias)


def f_gdn_post(yf, yb, z, ng):
    n = yf.shape[1] // HEAD128
    return (_group_apply(yf + yb, HEAD128, _rms) * jnp.concatenate([ng] * n, axis=1) * _silu(z),)


def f_loss(x, tgt, fg):
    err = jnp.square(_rms(x) * fg - tgt)
    return (jnp.broadcast_to(jnp.mean(err, axis=-1, keepdims=True), (x.shape[0], LANES)),)


def _par(v):
    return v.reshape(1, 1, -1)


def _shift(u, lc):
    z = jnp.zeros((1, u.shape[1]), u.dtype)
    prev = jnp.concatenate([z, u[:lc - 1], z, u[lc:-1]], axis=0)
    nxt = jnp.concatenate([u[1:lc], z, u[lc + 1:], z], axis=0)
    return prev, nxt


def _grp(a, g_n):
    t = a.shape[0]
    return a.reshape(t, g_n, -1).transpose(1, 0, 2)


def ssd_mixer(h, p, lc, start, carry):
    d = h.shape[1]
    di = 2 * d
    gn = SSD_GROUPS * SSD_STATE
    heads = di // SSD_HEAD_DIM
    r_n = heads // SSD_GROUPS
    u, got_in = cmm(h, p["ssd_w_in"], "ssd_in", carry[0])
    z, xbc, dtr = u[:, :di], u[:, di:di + di + 2 * gn], u[:, 2 * di + 2 * gn:]
    up, un = _shift(xbc, lc)
    cw = p["ssd_conv_w"]
    (xbc,) = rowop(f_conv_bias, "ssd_conv", [up, xbc, un], [_par(cw[0]), _par(cw[1]), _par(cw[2]), _par(p["ssd_conv_b"])],
                   [xbc.shape[1]], 128)
    xs, bm, cm = xbc[:, :di], xbc[:, di:di + gn], xbc[:, di + gn:]
    dt, la = rowop(f_dt, "ssd_dt", [dtr], [_par(p["ssd_dt_bias"]), _par(p["ssd_a_log"])], [2 * heads, 2 * heads], 256)

    def scan(tag, rev, grps):
        return scanop(scalar_decay_chunk(r_n, SSD_STATE, SSD_HEAD_DIM, True, rev), "ssd_scan_" + tag, [cm, bm, xs],
                      [SSD_STATE, SSD_STATE, r_n * SSD_HEAD_DIM], grps, (r_n * SSD_STATE, SSD_HEAD_DIM),
                      r_n * SSD_HEAD_DIM, rev, lc)

    yf = scan("f", False, [_grp(la[:, :heads], SSD_GROUPS), _grp(dt[:, :heads], SSD_GROUPS)])
    yb = scan("b", True, [_grp(la[:, heads:], SSD_GROUPS), _grp(dt[:, heads:], SSD_GROUPS)])
    (yn,) = rowop(f_ssd_post, "ssd_post", [yf[start:], yb[start:], xs[start:], z[start:]],
                  [_par(p["ssd_d"]), _par(p["ssd_norm_g"])], [di], 128)
    y, got_out = cmm(yn, p["ssd_w_out"], "ssd_out", carry[1])
    return y, got_in + got_out


def _rope_tables(n_lat, lc, d, heads):
    qk = d // heads
    half = qk // 2
    quarter = half // 2
    pos = jnp.arange(n_lat)
    inv_freq = ROPE_BASE ** (-jnp.arange(0, half, 2, dtype=F32) / half)

    def tab(p):
        ang = p.astype(F32)[:, None] * inv_freq
        return jnp.cos(ang), jnp.sin(ang)

    cr, sr = tab(pos // GRID_W)
    cc, sc = tab(pos % GRID_W)
    cos = jnp.concatenate([cr, cr, cc, cc], axis=1)
    sin = jnp.concatenate([-sr, sr, -sc, sc], axis=1)
    cos = jnp.concatenate([jnp.ones((lc, qk), F32), cos], axis=0)
    sin = jnp.concatenate([jnp.zeros((lc, qk), F32), sin], axis=0)
    return jnp.tile(cos, (1, heads)), jnp.tile(sin, (1, heads)), quarter


def retention_mixer(h, p, lc, start, carry):
    t, d = h.shape
    dv = 2 * d
    qk = d // RET_HEADS
    u, got_in = cmm(h, p["ret_w_in"], "ret_in", carry[0])
    q, k, v, g = u[:, :d], u[:, d:2 * d], u[:, 2 * d:2 * d + dv], u[:, 2 * d + dv:]
    cos, sin, quarter = _rope_tables(t - lc, lc, d, RET_HEADS)
    cos, sin = lax.stop_gradient(cos), lax.stop_gradient(sin)

    def swap(a):
        return a.reshape(t, -1, 2, quarter)[:, :, ::-1, :].reshape(t, d)

    (q,) = rowop(f_rope(1.0), "ret_rope_q", [q, swap(q), cos, sin], [], [d], 256)
    (k,) = rowop(f_rope(qk ** -0.5), "ret_rope_k", [k, swap(k), cos, sin], [], [d], 256)

    def scan(tag, rev, grps):
        return scanop(scalar_decay_chunk(1, qk, dv // RET_HEADS, False, rev), "ret_scan_" + tag, [q, k, v],
                      [qk, qk, dv // RET_HEADS], grps, (qk, dv // RET_HEADS), dv // RET_HEADS, rev, lc, GROUPS_PER_STEP)

    ld = p["ret_log_decay"]
    la_f = jnp.broadcast_to(ld[0][:, None, None], (RET_HEADS, t, 1))
    la_b = jnp.broadcast_to(ld[1][:, None, None], (RET_HEADS, t, 1))
    yf, yb = scan("f", False, [la_f]), scan("b", True, [la_b])
    (yn,) = rowop(f_ret_post, "ret_post", [yf[start:], yb[start:], g[start:]], [], [dv], 128)
    y, got_out = cmm(yn, p["ret_w_out"], "ret_out", carry[1])
    return y, got_in + got_out


def hgrn2_mixer(h, p, lc, start, layer, carry):
    t, d = h.shape
    u, got_in = cmm(h, p["hgrn_w_in"], "hgrn_in", carry[0])
    q, f_f, f_b, i, g = (u[:, j * d:(j + 1) * d] for j in range(5))
    (lb,) = rowop(f_lower_bound(layer), "hgrn_lb", [p["hgrn_lb_logits"]], [], [d], DEPTH)
    lf_f, k_f, lf_b, k_b = rowop(f_hgrn_gates, "hgrn_gates", [f_f, f_b], [lb[0:1][None]], [d] * 4, 256)

    def scan(tag, rev, cols):
        return scanop(vector_decay_chunk(rev), "hgrn_scan_" + tag, cols, [HEAD128] * 4, [], (HEAD128, HEAD128), HEAD128,
                      rev, lc, GROUPS_PER_STEP)

    yf = scan("f", False, [q, k_f, i, lf_f])
    yb = scan("b", True, [q, k_b, i, lf_b])
    (yn,) = rowop(f_hgrn_post, "hgrn_post", [yf[start:], yb[start:], g[start:]], [_par(p["hgrn_norm_g"])], [d], 256)
    y, got_out = cmm(yn, p["hgrn_w_out"], "hgrn_out", carry[1])
    return y, got_in + got_out


def gdn_mixer(h, p, lc, start):
    t, d = h.shape
    dk, dv = d, 2 * d
    kh = d // HEAD128
    r_n = 2
    cc = 2 * dk + dv
    u = pmatmul(h, p["gdn_w_in"][:, :cc + dv], "gdn_in")
    ug = pmatmul(h, p["gdn_w_in"][:, cc + dv:], "gdn_in_gate")
    qkv, z = u[:, :cc], u[:, cc:]
    nb = 2 * kh * r_n
    bt, a = ug[:, :nb], ug[:, nb:]
    up, un = _shift(qkv, lc)
    cw = p["gdn_conv_w"]
    (qkv,) = rowop(f_conv, "gdn_conv", [up, qkv, un], [_par(cw[0]), _par(cw[1]), _par(cw[2])], [cc], 128)
    q, k, v = qkv[:, :dk], qkv[:, dk:2 * dk], qkv[:, 2 * dk:]
    q, k = rowop(f_gdn_qk, "gdn_qk", [q, k], [], [dk, dk], 256)
    beta, la = rowop(f_gdn_gates, "gdn_gates", [bt, a], [_par(p["gdn_dt_bias"]), _par(p["gdn_a_log"])], [nb, nb], 256)

    def scan(tag, rev, grps):
        return scanop(delta_chunk(r_n, HEAD128, HEAD128, rev), "gdn_scan_" + tag, [q, k, v],
                      [HEAD128, HEAD128, r_n * HEAD128], grps, (r_n * HEAD128, HEAD128), r_n * HEAD128, rev, lc,
                      GROUPS_PER_STEP, True)

    half = nb // 2
    yf = scan("f", False, [_grp(beta[:, :half], kh), _grp(la[:, :half], kh)])
    yb = scan("b", True, [_grp(beta[:, half:], kh), _grp(la[:, half:], kh)])
    (yn,) = rowop(f_gdn_post, "gdn_post", [yf[start:], yb[start:], z[start:]], [_par(p["gdn_norm_g"])], [dv], 128)
    return pmatmul(yn, p["gdn_w_out"], "gdn_out")


@jax.custom_vjp
def tap(w, probe):
    return w


tap.defvjp(lambda w, probe: (w, None), lambda _, g: (jnp.zeros_like(g), g))


def cmm(a, w, name, carry):
    if carry:
        return pmatmul(a, w, name, carry)
    return pmatmul(a, w, name), []


MIXER_W = [("ssd_w_in", "ssd_w_out"), ("ret_w_in", "ret_w_out"), ("hgrn_w_in", "hgrn_w_out"), ("gdn_w_in", "gdn_w_out")]


def local_loss(x, ctx, mod2, tgt, p, shards, shapes):
    lc, d = ctx.shape
    p = dict(p)
    xc = jnp.concatenate([ctx, x], axis=0)
    for i in range(DEPTH):
        keep = i < DEPTH - 1
        start = 0 if keep else lc
        nxt = [MIXER_W[i + 1][0], MIXER_W[i + 1][1], "mlp_w1_%d" % (i + 1), "mlp_w2_%d" % (i + 1)] if keep else []
        carry = [[shards[n]] for n in nxt] if keep else [None] * 4
        sh1, sc1, g1, sh2, sc2, g2 = (mod2[i][:, j * d:(j + 1) * d][:, None, :] for j in range(6))
        (h,) = rowop(f_adaln, "adaln_a%d" % i, [xc], [_par(p["norm_g"][i, 0]), sh1, sc1], [d], 256, lc)
        if i % 4 == 0:
            y, got = ssd_mixer(h, p, lc, start, carry[:2])
        elif i % 4 == 1:
            y, got = retention_mixer(h, p, lc, start, carry[:2])
        elif i % 4 == 2:
            y, got = hgrn2_mixer(h, p, lc, start, i, carry[:2])
        else:
            y, got = gdn_mixer(h, p, lc, start), []
        if not keep:
            xc, g1, sh2, sc2, g2, lc = xc[lc:], g1[1:], sh2[1:], sc2[1:], g2[1:], 0
        (xc,) = rowop(f_resid, "resid_a%d" % i, [xc, y], [g1], [d], 256, lc)
        (h2,) = rowop(f_adaln, "adaln_b%d" % i, [xc], [_par(p["norm_g"][i, 1]), sh2, sc2], [d], 256, lc)
        a, got1 = cmm(h2, p["mlp_w1_%d" % i], "mlp1_%d" % i, carry[2])
        (a,) = rowop(f_relu2, "relu2_%d" % i, [a], [], [a.shape[1]], 128)
        m, got2 = cmm(a, p["mlp_w2_%d" % i], "mlp2_%d" % i, carry[3])
        (xc,) = rowop(f_resid, "resid_b%d" % i, [xc, m], [g2], [d], 256, lc)
        for n, g8 in zip(nxt, got + got1 + got2):
            shp, axis = shapes[n]
            p[n] = tap(_to_full(g8.reshape((N_DEV,) + shp), axis)[0], p[n])
    (rows,) = rowop(f_loss, "loss", [xc, tgt], [_par(p["final_g"])], [LANES], 256)
    return 0.5 * jnp.sum(rows[:, 0])


def exchange(x, name):
    def body(x_ref, o_ref, send_sems, recv_sems, local_sem):
        mx, my, mc = lax.axis_index("x"), lax.axis_index("y"), lax.axis_index("c")
        me = 4 * mx + 2 * my + mc
        copies = []
        for k in range(1, N_DEV):
            px = 1 - mx if (k >> 2) & 1 else mx
            py = 1 - my if (k >> 1) & 1 else my
            pc = 1 - mc if k & 1 else mc
            cp = pltpu.make_async_remote_copy(src_ref=x_ref, dst_ref=o_ref.at[me], send_sem=send_sems.at[k - 1],
                                              recv_sem=recv_sems.at[k - 1], device_id=(px, py, pc), device_id_type=MESH)
            cp.start()
            copies.append(cp)
        mine = pltpu.make_async_copy(x_ref, o_ref.at[me], local_sem)
        mine.start()
        for cp in copies:
            cp.wait_recv()
        for cp in copies:
            cp.wait_send()
        mine.wait()

    return pl.pallas_call(
        body, out_shape=SDS((N_DEV,) + x.shape, x.dtype),
        in_specs=[pl.BlockSpec(memory_space=pl.ANY)], out_specs=pl.BlockSpec(memory_space=pl.ANY),
        scratch_shapes=[pltpu.SemaphoreType.DMA((N_DEV - 1,)), pltpu.SemaphoreType.DMA((N_DEV - 1,)),
                        pltpu.SemaphoreType.DMA],
        name=name)(x)


def _place():
    mx, my, mc = lax.axis_index("x"), lax.axis_index("y"), lax.axis_index("c")
    chips = [(1 - mx, my), (mx, 1 - my), (1 - mx, 1 - my)]
    return mx, my, mc, chips


def _any_specs(n):
    return [pl.BlockSpec(memory_space=pl.ANY)] * n


def _gather_sems(n):
    return [pltpu.SemaphoreType.DMA((n, 7)), pltpu.SemaphoreType.DMA((n, 7)), pltpu.SemaphoreType.DMA((n,))]


def _gather_copies(x_refs, o_refs, send_sems, recv_sems, local_sems):
    mx, my, mc, chips = _place()
    me, sib = (mx, my, mc), (mx, my, 1 - mc)

    def copy(a, k, block, to, src=None):
        dst = o_refs[a].at[4 * block[0] + 2 * block[1] + block[2]]
        return pltpu.make_async_remote_copy(src_ref=dst if src is None else src, dst_ref=dst, send_sem=send_sems.at[a, k],
                                            recv_sem=recv_sems.at[a, k], device_id=to, device_id_type=MESH)

    n = len(x_refs)
    mine = [pltpu.make_async_copy(x_refs[a], o_refs[a].at[4 * mx + 2 * my + mc], local_sems.at[a]) for a in range(n)]
    own = [copy(a, 1 + j, me, (*chip, mc), src=x_refs[a]) for a in range(n) for j, chip in enumerate(chips)]
    own += [copy(a, 0, me, sib, src=x_refs[a]) for a in range(n)]
    return mc, chips, me, sib, copy, mine, own


def _gather_start(x_refs, o_refs, send_sems, recv_sems, local_sems):
    *_, mine, own = _gather_copies(x_refs, o_refs, send_sems, recv_sems, local_sems)
    for cp in mine + own:
        cp.start()


def _gather_finish(x_refs, o_refs, send_sems, recv_sems, local_sems):
    mc, chips, me, sib, copy, mine, own = _gather_copies(x_refs, o_refs, send_sems, recv_sems, local_sems)
    n = len(x_refs)
    passed = []
    for a in range(n):
        for j, chip in enumerate(chips):
            copy(a, 1 + j, (*chip, mc), me).wait_recv()
            fwd = copy(a, 4 + j, (*chip, mc), sib)
            fwd.start()
            passed.append(fwd)
    for a in range(n):
        copy(a, 0, sib, me).wait_recv()
        for j, chip in enumerate(chips):
            copy(a, 4 + j, (*chip, 1 - mc), me).wait_recv()
    for cp in own + passed:
        cp.wait_send()
    for cp in mine:
        cp.wait()


def gather_two_level(arrs, name):
    n = len(arrs)

    def body(*refs):
        _gather_start(refs[:n], refs[n:2 * n], *refs[2 * n:])
        _gather_finish(refs[:n], refs[n:2 * n], *refs[2 * n:])

    return pl.pallas_call(
        body, out_shape=[SDS((N_DEV,) + a.shape, a.dtype) for a in arrs], in_specs=_any_specs(n), out_specs=_any_specs(n),
        scratch_shapes=_gather_sems(n), name=name)(*arrs)


def scatter_sibling(parts, name):
    n = len(parts)

    def body(*refs):
        x_refs, o_refs = refs[:n], refs[n:2 * n]
        send_sems, recv_sems = refs[2 * n:]
        mx, my, mc, _ = _place()
        copies = []
        for a in range(n):
            for b in range(4):
                cp = pltpu.make_async_remote_copy(src_ref=x_refs[a].at[b, 1 - mc], dst_ref=o_refs[a].at[b],
                                                  send_sem=send_sems.at[a, b], recv_sem=recv_sems.at[a, b],
                                                  device_id=(mx, my, 1 - mc), device_id_type=MESH)
                cp.start()
                copies.append(cp)
        for cp in copies:
            cp.wait_recv()
        for cp in copies:
            cp.wait_send()

    return pl.pallas_call(
        body, out_shape=[SDS((4,) + p.shape[2:], p.dtype) for p in parts], in_specs=_any_specs(n), out_specs=_any_specs(n),
        scratch_shapes=[pltpu.SemaphoreType.DMA((n, 4)), pltpu.SemaphoreType.DMA((n, 4))], name=name)(*parts)


def pair_sum(mine, theirs, name):
    _, rows, cols = mine.shape
    tr = _pick(rows, max(8, (1 << 19) // cols), 8)

    def body(a_ref, b_ref, o_ref):
        o_ref[...] = (a_ref[...].astype(F32) + b_ref[...].astype(F32)).astype(o_ref.dtype)

    spec = pl.BlockSpec((None, tr, cols), lambda b, i: (b, i, 0))
    return pl.pallas_call(body, grid=(4, rows // tr), in_specs=[spec, spec], out_specs=spec,
                          out_shape=SDS(mine.shape, mine.dtype), compiler_params=_cp("parallel", "parallel"), name=name)(mine, theirs)


def scatter_chips(sums, name):
    n = len(sums)

    def body(*refs):
        x_refs, o_refs = refs[:n], refs[n:2 * n]
        send_sems, recv_sems, local_sems = refs[2 * n:]
        mx, my, mc, chips = _place()
        my_chip = 2 * mx + my
        mine = [pltpu.make_async_copy(x_refs[a].at[my_chip], o_refs[a].at[my_chip], local_sems.at[a]) for a in range(n)]
        for cp in mine:
            cp.start()
        copies = []
        for a in range(n):
            for j, (px, py) in enumerate(chips):
                cp = pltpu.make_async_remote_copy(src_ref=x_refs[a].at[2 * px + py], dst_ref=o_refs[a].at[my_chip],
                                                  send_sem=send_sems.at[a, j], recv_sem=recv_sems.at[a, j],
                                                  device_id=(px, py, mc), device_id_type=MESH)
                cp.start()
                copies.append(cp)
        for cp in copies:
            cp.wait_recv()
        for cp in copies:
            cp.wait_send()
        for cp in mine:
            cp.wait()

    return pl.pallas_call(
        body, out_shape=[SDS(p.shape, p.dtype) for p in sums], in_specs=_any_specs(n), out_specs=_any_specs(n),
        scratch_shapes=[pltpu.SemaphoreType.DMA((n, 3)), pltpu.SemaphoreType.DMA((n, 3)), pltpu.SemaphoreType.DMA((n,))],
        name=name)(*sums)


def sum_parts(parts, name):
    n_p, rows, _ = parts.shape
    tr = _pick(rows, 1024, 8)

    def body(p_ref, o_ref):
        acc = p_ref[0].astype(F32)
        for j in range(1, n_p):
            acc = acc + p_ref[j].astype(F32)
        o_ref[...] = acc

    return pl.pallas_call(body, grid=(rows // tr,), in_specs=[pl.BlockSpec((n_p, tr, LANES), lambda i: (0, i, 0))],
                          out_specs=pl.BlockSpec((tr, LANES), lambda i: (i, 0)), out_shape=SDS((rows, LANES), F32),
                          compiler_params=_cp("parallel"), name=name)(parts)


def adamw(w, parts, m, v, name):
    rows, cols = w.shape
    n_p = parts.shape[0]
    tr = _pick(rows, max(8, (1 << 18) // cols), 8)
    c1 = 1.0 - ADAM_B1 ** ADAM_STEP
    c2 = 1.0 - ADAM_B2 ** ADAM_STEP

    def body(w_ref, p_ref, m_ref, v_ref, g_out, d_out, m_out, v_out):
        g = p_ref[0].astype(F32)
        for j in range(1, n_p):
            g = g + p_ref[j].astype(F32)
        m_new = ADAM_B1 * m_ref[...] + (1.0 - ADAM_B1) * g
        v_new = ADAM_B2 * v_ref[...] + (1.0 - ADAM_B2) * jnp.square(g)
        g_out[...] = g
        m_out[...] = m_new
        v_out[...] = v_new
        d_out[...] = -ADAM_LR * ((m_new / c1) / (jnp.sqrt(v_new / c2) + ADAM_EPS) + ADAM_WD * w_ref[...])

    spec = pl.BlockSpec((tr, cols), lambda i: (i, 0))
    return pl.pallas_call(
        body, grid=(rows // tr,), in_specs=[spec, pl.BlockSpec((n_p, tr, cols), lambda i: (0, i, 0)), spec, spec],
        out_specs=[spec] * 4, out_shape=[SDS((rows, cols), F32)] * 4, compiler_params=_cp("parallel"), name=name)(w, parts, m, v)


FWD_NAMES = ["x", "c", "ctx", "c_ctx", "ada_w", "ada_b", "norm_g", "mlp_w1", "mlp_w2", "final_g", "ssd_w_in", "ssd_conv_w",
             "ssd_conv_b", "ssd_dt_bias", "ssd_a_log", "ssd_d", "ssd_norm_g", "ssd_w_out", "ret_w_in", "ret_log_decay",
             "ret_w_out", "hgrn_w_in", "hgrn_lb_logits", "hgrn_norm_g", "hgrn_w_out", "gdn_w_in", "gdn_conv_w", "gdn_dt_bias",
             "gdn_a_log", "gdn_norm_g", "gdn_w_out"]
WEIGHTS = FWD_NAMES[3:]
BIG = {"mlp_w1": 2, "mlp_w2": 1, "ssd_w_in": 2, "ssd_w_out": 1, "ret_w_in": 2, "ret_w_out": 1, "hgrn_w_in": 2,
       "hgrn_w_out": 1, "gdn_w_in": 2, "gdn_w_out": 1}
SMALL_SHARDED = ["norm_g", "ssd_conv_w", "gdn_conv_w", "hgrn_norm_g"]
SMALL = [n for n in WEIGHTS if n not in BIG and n != "ada_w"]


def _to_full(g8, axis):
    _, l, a, b = g8.shape
    if axis == 1:
        return g8.transpose(1, 0, 2, 3).reshape(l, N_DEV * a, b)
    return g8.transpose(1, 2, 0, 3).reshape(l, a, N_DEV * b)


def _to_shards(full, axis):
    l, a, b = full.shape
    if axis == 1:
        return full.reshape(l, N_DEV, a // N_DEV, b).transpose(1, 0, 2, 3)
    return full.reshape(l, a, N_DEV, b // N_DEV).transpose(2, 0, 1, 3)


def _pack_small(arrs):
    parts, meta, off = [], [], 0
    for a in arrs:
        n = a.size
        npad = -(-n // LANES) * LANES
        parts.append(jnp.pad(a.reshape(-1).astype(F32), (0, npad - n)))
        meta.append((off, n, a.shape))
        off += npad
    rows = -(-(off // LANES) // 8) * 8
    flat = jnp.concatenate(parts)
    flat = jnp.pad(flat, (0, rows * LANES - off))
    return flat.reshape(rows, LANES), meta


def _unpack_small(buf, meta):
    flat = buf.reshape(buf.shape[:-2] + (-1,))
    return [flat[..., off:off + n].reshape(buf.shape[:-2] + tuple(shape)) for off, n, shape in meta]


def _my_shard(full, me, n_local):
    return lax.dynamic_slice_in_dim(full, me * n_local, n_local, axis=full.ndim - 1)


def kernel(*args):
    n_f = len(FWD_NAMES)
    inp = dict(zip(FWD_NAMES, args[:n_f]))
    tgt = args[n_f][0]
    n_w = len(WEIGHTS)
    mom_m = dict(zip(WEIGHTS, args[n_f + 1:n_f + 1 + n_w]))
    mom_v = dict(zip(WEIGHTS, args[n_f + 1 + n_w:n_f + 1 + 2 * n_w]))
    x, ctx, c = inp["x"][0], inp["ctx"][0], inp["c"]
    d = x.shape[1]
    me = 4 * lax.axis_index("x") + 2 * lax.axis_index("y") + lax.axis_index("c")

    buf, meta = _pack_small([c] + [inp[n] for n in SMALL_SHARDED])
    got = _unpack_small(exchange(buf, "gather_small"), meta)
    c_all = got[0].reshape(N_DEV, d)
    small_full = {}
    for n, g8 in zip(SMALL_SHARDED, got[1:]):
        small_full[n] = jnp.moveaxis(g8, 0, -2).reshape(g8.shape[1:-1] + (N_DEV * g8.shape[-1],))
    cond_in = jnp.concatenate([c_all, inp["c_ctx"][None]], axis=0)
    cond = _silu(cond_in)
    cond16 = jnp.pad(cond, ((0, 16 - cond.shape[0]), (0, 0)))

    ada_w = inp["ada_w"]
    n_ada = ada_w.shape[2]
    mod_loc = jnp.stack([_mm(cond16, ada_w[i], "nn", "ada_fwd")[:N_DEV + 1] for i in range(DEPTH)])
    mbuf, mmeta = _pack_small([mod_loc])
    (mod8,) = _unpack_small(exchange(mbuf, "gather_mod"), mmeta)
    mod_full = mod8.transpose(1, 2, 0, 3).reshape(DEPTH, N_DEV + 1, N_DEV * n_ada) + inp["ada_b"][:, None, :]
    mod2 = jnp.stack([mod_full[:, N_DEV], lax.dynamic_index_in_dim(mod_full, me, axis=1, keepdims=False)], axis=1)

    big = list(BIG)
    shards, shapes = {}, {}
    for n in big:
        w16 = inp[n].astype(BF16)
        for i in range(w16.shape[0]):
            key = "%s_%d" % (n, i) if w16.shape[0] > 1 else n
            shards[key], shapes[key] = w16[i], ((1,) + w16.shape[1:], BIG[n])
    first = ["ssd_w_in", "ssd_w_out", "mlp_w1_0", "mlp_w2_0"]
    p = {}
    for n, g8 in zip(first, gather_two_level([shards[n] for n in first], "gather_weights")):
        p[n] = _to_full(g8.reshape((N_DEV,) + shapes[n][0]), shapes[n][1])[0]
    later = {n: s for n, s in shards.items() if n not in first}
    for n, s in later.items():
        rows, cols = s.shape
        p[n] = jnp.zeros((N_DEV * rows, cols) if shapes[n][1] == 1 else (rows, N_DEV * cols), BF16)
    for n in SMALL:
        if n not in ("c_ctx", "ada_b"):
            p[n] = small_full[n] if n in small_full else inp[n]
    for n in ("ssd_conv_w", "ssd_conv_b", "ssd_dt_bias", "ssd_a_log", "ssd_d", "ssd_norm_g", "ret_log_decay", "hgrn_norm_g",
              "gdn_conv_w", "gdn_dt_bias", "gdn_a_log", "gdn_norm_g"):
        p[n] = p[n][0]

    loss_loc, (g_mod2, g_x, g_p) = jax.value_and_grad(
        lambda mod2_, x_, p_: local_loss(x_, ctx, mod2_, tgt, p_, later, shapes), argnums=(0, 1, 2))(mod2, x, p)

    small_g_names = [n for n in SMALL if n not in ("c_ctx", "ada_b")]
    gbuf, gmeta = _pack_small([loss_loc.reshape(1), g_mod2] + [g_p[n] for n in small_g_names])
    g8 = exchange(gbuf, "gather_small_grads")
    gsum = _unpack_small(sum_parts(g8, "sum_small_grads"), gmeta)
    loss = gsum[0][0]
    g_small = dict(zip(small_g_names, gsum[2:]))
    for n in ("ssd_conv_w", "ssd_conv_b", "ssd_dt_bias", "ssd_a_log", "ssd_d", "ssd_norm_g", "ret_log_decay", "hgrn_norm_g",
              "gdn_conv_w", "gdn_dt_bias", "gdn_a_log", "gdn_norm_g"):
        g_small[n] = g_small[n][None]
    dmod_each = _unpack_small(g8, gmeta)[1]
    dmod9 = jnp.concatenate([dmod_each[:, :, 1].transpose(1, 0, 2), gsum[1][:, 0:1]], axis=1)
    g_small["ada_b"] = gsum[1][:, 0] + gsum[1][:, 1]
    dmod16 = jnp.pad(_my_shard(dmod9, me, n_ada), ((0, 0), (0, 16 - dmod9.shape[1]), (0, 0)))
    g_ada_w = jnp.stack([_mm(cond16, dmod16[i], "tn", "ada_dw") for i in range(DEPTH)])
    dcond_part = _mm(dmod16[0], ada_w[0], "nt", "ada_dx")
    for i in range(1, DEPTH):
        dcond_part = dcond_part + _mm(dmod16[i], ada_w[i], "nt", "ada_dx")
    cbuf, cmeta = _pack_small([dcond_part[N_DEV]])
    (dcond8,) = _unpack_small(sum_parts(exchange(cbuf, "gather_dcond"), "sum_dcond"), cmeta)
    g_small["c_ctx"] = jax.vjp(_silu, inp["c_ctx"])[1](dcond8)[0]

    my_core = lax.axis_index("c")
    blocks = []
    for n in big:
        layers = inp[n].shape[0]
        gf = jnp.stack([g_p["%s_%d" % (n, i)] for i in range(layers)]) if layers > 1 else g_p[n][None]
        blocks.append(_to_shards(gf, BIG[n]).reshape(4, 2, -1, inp[n].shape[2]))
    from_sibling = scatter_sibling(blocks, "scatter_grads_sibling")
    sums = [pair_sum(lax.dynamic_index_in_dim(b, my_core, axis=1, keepdims=False), t, "pair_sum_" + n)
            for n, b, t in zip(big, blocks, from_sibling)]
    parts = scatter_chips(sums, "scatter_grads_chips")

    out = {}

    def view(a):
        return a.reshape(-1, a.shape[-1])

    for n, pr in zip(big, parts):
        res = adamw(view(inp[n]), pr, view(mom_m[n]), view(mom_v[n]), "adamw_" + n)
        out[n] = [r.reshape(inp[n].shape) for r in res]
    res = adamw(view(ada_w), view(g_ada_w)[None], view(mom_m["ada_w"]), view(mom_v["ada_w"]), "adamw_ada_w")
    out["ada_w"] = [r.reshape(ada_w.shape) for r in res]
    for n in SMALL_SHARDED:
        g_small[n] = _my_shard(g_small[n], me, inp[n].shape[-1])
    wb, wmeta = _pack_small([inp[n] for n in SMALL])
    gb, _ = _pack_small([g_small[n] for n in SMALL])
    mb, _ = _pack_small([mom_m[n] for n in SMALL])
    vb, _ = _pack_small([mom_v[n] for n in SMALL])
    res = [_unpack_small(r, wmeta) for r in adamw(wb, gb[None], mb, vb, "adamw_small")]
    for j, n in enumerate(SMALL):
        out[n] = [r[j] for r in res]

    outs = [loss, g_x[None]]
    for k in range(4):
        outs += [out[n][k] for n in WEIGHTS]
    return tuple(outs)
```

```python
import functools
import math

import jax
import jax.numpy as jnp
from jax import lax
from jax.experimental import pallas as pl
from jax.experimental.pallas import tpu as pltpu

F32 = jnp.float32
BF16 = jnp.bfloat16
HI = lax.Precision.HIGHEST

N_DEV = 8
CHUNK = 64
SUB = 16
GROUPS_PER_STEP = 2
GRID_W = 64
ROPE_BASE = 10000.0
NORM_EPS = 1e-6
DEPTH = 4
SSD_GROUPS = 8
SSD_STATE = 128
SSD_HEAD_DIM = 64
RET_HEADS = 8
HEAD128 = 128
ADAM_LR, ADAM_B1, ADAM_B2, ADAM_EPS, ADAM_WD, ADAM_STEP = 0.001, 0.9, 0.999, 1e-08, 0.01, 10

V7X_VMEM_BYTES = 64 * 1024 * 1024
VMEM_LIMIT = (V7X_VMEM_BYTES * 3) // 4
ROW_PIPELINE_BYTES = V7X_VMEM_BYTES // 4
LANES = 128
MESH = pl.DeviceIdType.MESH

SDS = jax.ShapeDtypeStruct


def _cp(*sem):
    return pltpu.CompilerParams(dimension_semantics=tuple(sem), vmem_limit_bytes=VMEM_LIMIT)


def _pick(n, cap, quantum=LANES):
    best = None
    d = quantum
    while d <= min(n, cap):
        if n % d == 0:
            best = d
        d += quantum
    return n if best is None else best


def _mm(a, b, mode, name, out_dtype=F32, carry=()):
    if mode == "nn":
        (m, k), n = a.shape, b.shape[1]
    elif mode == "nt":
        (m, k), n = a.shape, b.shape[0]
    else:
        (k, m), n = a.shape, b.shape[1]
    tm, tn, tk = _pick(m, 1024), _pick(n, 1280), _pick(k, 2048)
    nk = k // tk
    if mode == "nn":
        a_spec = pl.BlockSpec((tm, tk), lambda i, j, kk: (i, kk))
        b_spec = pl.BlockSpec((tk, tn), lambda i, j, kk: (kk, j))
        dims = (((1,), (0,)), ((), ()))
    elif mode == "nt":
        a_spec = pl.BlockSpec((tm, tk), lambda i, j, kk: (i, kk))
        b_spec = pl.BlockSpec((tn, tk), lambda i, j, kk: (j, kk))
        dims = (((1,), (1,)), ((), ()))
    else:
        a_spec = pl.BlockSpec((tk, tm), lambda i, j, kk: (kk, i))
        b_spec = pl.BlockSpec((tk, tn), lambda i, j, kk: (kk, j))
        dims = (((0,), (0,)), ((), ()))

    n_x = len(carry)
    ni, nj = m // tm, n // tn

    def body(*refs):
        a_ref, b_ref = refs[:2]
        x_refs = refs[2:2 + n_x]
        o_ref = refs[2 + n_x]
        g_refs = refs[3 + n_x:3 + 2 * n_x]
        acc_ref = refs[3 + 2 * n_x]
        sems = refs[4 + 2 * n_x:]
        i, j, kk = pl.program_id(0), pl.program_id(1), pl.program_id(2)

        if n_x:
            @pl.when((i == 0) & (j == 0) & (kk == 0))
            def _():
                _gather_start(x_refs, g_refs, *sems)

        @pl.when(kk == 0)
        def _():
            acc_ref[...] = jnp.zeros_like(acc_ref)

        acc_ref[...] += lax.dot_general(a_ref[...].astype(BF16), b_ref[...].astype(BF16), dims,
                                        preferred_element_type=F32)

        @pl.when(kk == nk - 1)
        def _():
            o_ref[...] = acc_ref[...].astype(o_ref.dtype)

        if n_x:
            @pl.when((i == ni - 1) & (j == nj - 1) & (kk == nk - 1))
            def _():
                _gather_finish(x_refs, g_refs, *sems)

    res = pl.pallas_call(
        body, grid=(ni, nj, nk), in_specs=[a_spec, b_spec] + _any_specs(n_x),
        out_specs=[pl.BlockSpec((tm, tn), lambda i, j, kk: (i, j))] + _any_specs(n_x),
        out_shape=[SDS((m, n), out_dtype)] + [SDS((N_DEV,) + x.shape, x.dtype) for x in carry],
        scratch_shapes=[pltpu.VMEM((tm, tn), F32)] + (_gather_sems(n_x) if n_x else []),
        compiler_params=_cp("arbitrary", "arbitrary", "arbitrary") if n_x else _cp("parallel", "parallel", "arbitrary"),
        name=name)(a, b, *carry)
    return (res[0], list(res[1:])) if n_x else res[0]


def pmatmul(a, w, name, carry=None):
    if carry:
        @jax.custom_vjp
        def cop(a, w, *xs):
            y, got = _mm(a, w, "nn", name + "_fwd", carry=xs)
            return y, tuple(got)

        def cfwd(a, w, *xs):
            return cop(a, w, *xs), (a, w, xs)

        def cbwd(res, cts):
            a, w, xs = res
            return (_mm(cts[0], w, "nt", name + "_dx"), _mm(a, cts[0], "tn", name + "_dw", out_dtype=w.dtype),
                    *[jnp.zeros_like(x) for x in xs])

        cop.defvjp(cfwd, cbwd)
        y, got = cop(a, w, *carry)
        return y, list(got)

    @jax.custom_vjp
    def op(a, w):
        return _mm(a, w, "nn", name + "_fwd")

    def fwd(a, w):
        return _mm(a, w, "nn", name + "_fwd"), (a, w)

    def bwd(res, g):
        a, w = res
        return _mm(g, w, "nt", name + "_dx"), _mm(a, g, "tn", name + "_dw", out_dtype=w.dtype)

    op.defvjp(fwd, bwd)
    return op(a, w)


def rowop(f, name, rows, pars, out_ws, tile, ctx_rows=0):
    t = rows[0].shape[0]
    tile = math.gcd(math.gcd(tile, t), ctx_rows if ctx_rows else t)
    row_bytes = 2 * 4 * (2 * sum(r.shape[1] for r in rows) + sum(out_ws))
    while tile > 8 and tile % 2 == 0 and tile * row_bytes > ROW_PIPELINE_BYTES:
        tile //= 2
    nt = t // tile
    cut = ctx_rows // tile
    n_r, n_p, n_o = len(rows), len(pars), len(out_ws)
    segs = [p.shape[0] for p in pars]
    assert all(s in (1, 2) for s in segs)

    def par_map(s):
        if s == 1:
            return lambda i: (0, 0, 0)
        return lambda i: (jnp.where(i >= cut, 1, 0), 0, 0)

    row_specs = [pl.BlockSpec((tile, r.shape[1]), lambda i: (i, 0)) for r in rows]
    par_specs = [pl.BlockSpec((None, 1, p.shape[2]), par_map(p.shape[0])) for p in pars]
    out_specs = [pl.BlockSpec((tile, w), lambda i: (i, 0)) for w in out_ws]
    out_shape = [SDS((t, w), F32) for w in out_ws]

    def fwd_call(*args):
        def body(*refs):
            outs = f(*[r[...] for r in refs[:n_r + n_p]])
            for o_ref, o in zip(refs[n_r + n_p:], outs):
                o_ref[...] = o

        return tuple(pl.pallas_call(body, grid=(nt,), in_specs=row_specs + par_specs, out_specs=out_specs,
                                    out_shape=out_shape, compiler_params=_cp("parallel"), name=name + "_fwd")(*args))

    def bwd_call(args, gouts):
        def body(*refs):
            ins = [r[...] for r in refs[:n_r + n_p]]
            gs = tuple(r[...] for r in refs[n_r + n_p:n_r + n_p + n_o])
            d_refs = refs[n_r + n_p + n_o:]
            _, vjp = jax.vjp(f, *ins)
            grads = vjp(gs)
            for ref, g in zip(d_refs[:n_r], grads[:n_r]):
                ref[...] = g
            i = pl.program_id(0)
            for ref, g, s in zip(d_refs[n_r:], grads[n_r:], segs):
                first = (i == 0) if s == 1 else jnp.logical_or(i == 0, i == cut)

                @pl.when(first)
                def _(ref=ref, g=g):
                    ref[...] = g

                @pl.when(jnp.logical_not(first))
                def _(ref=ref, g=g):
                    ref[...] += g

        d_specs = row_specs + par_specs
        d_shape = [SDS(r.shape, F32) for r in rows] + [SDS(p.shape, F32) for p in pars]
        return tuple(pl.pallas_call(body, grid=(nt,), in_specs=row_specs + par_specs + out_specs, out_specs=d_specs,
                                    out_shape=d_shape, compiler_params=_cp("arbitrary"), name=name + "_bwd")(*args, *gouts))

    @jax.custom_vjp
    def op(*args):
        return fwd_call(*args)

    op.defvjp(lambda *args: (fwd_call(*args), args), bwd_call)
    return op(*rows, *pars)


def scanop(chunk_fn, name, cols, col_ws, grps, state_shape, y_w, rev=False, ctx_rows=0, unroll=1, lockstep=False,
           carry=None):
    carry = list(carry or [])
    n_x = len(carry)
    t = cols[0].shape[0]
    g_n = cols[0].shape[1] // col_ws[0]
    u_n = unroll if g_n % unroll == 0 else 1
    gs_n = g_n // u_n
    nc = t // CHUNK
    ncx = ctx_rows // CHUNK
    n_c, n_g = len(cols), len(grps)

    def order(c):
        if not rev:
            return c
        return jnp.where(c < ncx, ncx - 1 - c, nc - 1 - (c - ncx))

    def specs(cmap):
        col_specs = [pl.BlockSpec((CHUNK, u_n * w), lambda g, c: (cmap(c), g)) for w in col_ws]
        grp_specs = [pl.BlockSpec((u_n, CHUNK, a.shape[2]), lambda g, c: (g, cmap(c), 0)) for a in grps]
        y_spec = pl.BlockSpec((CHUNK, u_n * y_w), lambda g, c: (cmap(c), g))
        s_spec = pl.BlockSpec((u_n, None) + state_shape, lambda g, c: (g, cmap(c), 0, 0))
        return col_specs, grp_specs, y_spec, s_spec

    def group_ins(ins, u):
        return ([a[:, u * w:(u + 1) * w] for a, w in zip(ins[:n_c], col_ws)] + [a[u] for a in ins[n_c:]])

    if lockstep:
        step_fn = chunk_fn
    else:
        def step_fn(s_list, ins_list):
            return [chunk_fn(s, *ins) for s, ins in zip(s_list, ins_list)]

    def fwd_call(*args):
        col_specs, grp_specs, y_spec, s_spec = specs(order)

        def body(*refs):
            n_in = n_c + n_g
            ins = [r[...] for r in refs[:n_in]]
            x_refs = refs[n_in:n_in + n_x]
            y_ref, sall_ref = refs[n_in + n_x:n_in + n_x + 2]
            g_refs = refs[n_in + n_x + 2:n_in + 2 * n_x + 2]
            s_ref = refs[n_in + 2 * n_x + 2]
            sems = refs[n_in + 2 * n_x + 3:]

            if n_x:
                @pl.when((pl.program_id(0) == 0) & (pl.program_id(1) == 0))
                def _():
                    _gather_start(x_refs, g_refs, *sems)

            @pl.when(pl.program_id(1) == 0)
            def _():
                s_ref[...] = jnp.zeros_like(s_ref)

            s = s_ref[...]
            sall_ref[...] = s
            res = step_fn([s[u] for u in range(u_n)], [group_ins(ins, u) for u in range(u_n)])
            y_ref[...] = jnp.concatenate([y for _, y in res], axis=1)
            s_ref[...] = jnp.stack([s_new for s_new, _ in res])

            if n_x:
                @pl.when((pl.program_id(0) == gs_n - 1) & (pl.program_id(1) == nc - 1))
                def _():
                    _gather_finish(x_refs, g_refs, *sems)

        return pl.pallas_call(
            body, grid=(gs_n, nc), in_specs=col_specs + grp_specs + _any_specs(n_x),
            out_specs=[y_spec, s_spec] + _any_specs(n_x),
            out_shape=[SDS((t, g_n * y_w), F32), SDS((g_n, nc) + state_shape, F32)]
            + [SDS((N_DEV,) + x.shape, x.dtype) for x in carry],
            scratch_shapes=[pltpu.VMEM((u_n,) + state_shape, F32)] + (_gather_sems(n_x) if n_x else []),
            compiler_params=_cp("arbitrary" if n_x else "parallel", "arbitrary"), name=name + "_fwd")(*args)

    def bwd_call(res, ct):
        args, s_all = res
        gy = ct[0] if n_x else ct
        args, xs = args[:n_c + n_g], args[n_c + n_g:]
        col_specs, grp_specs, y_spec, s_spec = specs(lambda c: order(nc - 1 - c))

        def body(*refs):
            ins = [r[...] for r in refs[:n_c + n_g]]
            s_prev = refs[n_c + n_g][...]
            dy = refs[n_c + n_g + 1][...]
            d_refs = refs[n_c + n_g + 2:-1]
            ds_ref = refs[-1]

            @pl.when(pl.program_id(1) == 0)
            def _():
                ds_ref[...] = jnp.zeros_like(ds_ref)

            ds = ds_ref[...]
            _, vjp = jax.vjp(step_fn, [s_prev[u] for u in range(u_n)], [group_ins(ins, u) for u in range(u_n)])
            d_s, d_ins = vjp([(ds[u], dy[:, u * y_w:(u + 1) * y_w]) for u in range(u_n)])
            grads = [[d_s[u]] + list(d_ins[u]) for u in range(u_n)]
            ds_ref[...] = jnp.stack([g[0] for g in grads])
            for j, ref in enumerate(d_refs):
                ref[...] = (jnp.concatenate([g[1 + j] for g in grads], axis=1) if j < n_c
                            else jnp.stack([g[1 + j] for g in grads]))

        grads = pl.pallas_call(
            body, grid=(gs_n, nc), in_specs=col_specs + grp_specs + [s_spec, y_spec], out_specs=col_specs + grp_specs,
            out_shape=[SDS(a.shape, F32) for a in args],
            scratch_shapes=[pltpu.VMEM((u_n,) + state_shape, F32)],
            compiler_params=_cp("parallel", "arbitrary"), name=name + "_bwd")(*args, s_all, gy)
        return tuple(grads) + tuple(jnp.zeros_like(x) for x in xs)

    def outputs(res):
        return (res[0], tuple(res[2:])) if n_x else res[0]

    @jax.custom_vjp
    def op(*args):
        return outputs(fwd_call(*args))

    def fwd(*args):
        res = fwd_call(*args)
        return outputs(res), (args, res[1])

    op.defvjp(fwd, bwd_call)
    out = op(*cols, *grps, *carry)
    return (out[0], list(out[1])) if n_x else out


def _dg(a, b, ca, cb, prec=None):
    return lax.dot_general(a, b, (((ca,), (cb,)), ((), ())), precision=prec, preferred_element_type=F32)


@functools.partial(jax.custom_vjp, nondiff_argnums=(2, 3))
def mmb(a, b, ca, cb):
    return _dg(a.astype(BF16), b.astype(BF16), ca, cb)


def _mmb_fwd(a, b, ca, cb):
    return mmb(a, b, ca, cb), (a, b)


def _mmb_bwd(ca, cb, res, g):
    a, b = res
    ab, bb, gb = a.astype(BF16), b.astype(BF16), g.astype(BF16)
    da = _dg(gb, bb, 1, 1 - cb) if ca == 1 else _dg(bb, gb, 1 - cb, 1)
    db = _dg(ab, gb, 1 - ca, 0) if cb == 0 else _dg(gb, ab, 0, 1 - ca)
    return da, db


mmb.defvjp(_mmb_fwd, _mmb_bwd)


def mmh(a, b, ca=1, cb=0):
    return _dg(a, b, ca, cb, HI)


def _masks(n, rev=False):
    i = lax.broadcasted_iota(jnp.int32, (n, n), 0)
    j = lax.broadcasted_iota(jnp.int32, (n, n), 1)
    return (i <= j, i < j) if rev else (i >= j, i > j)


def _silu(x):
    return x * jax.nn.sigmoid(x)


def _softplus(x):
    return jnp.maximum(x, 0.0) + jnp.log(1.0 + jnp.exp(-jnp.abs(x)))


def _expand_heads(p, width):
    h = p.shape[1]
    lane = lax.broadcasted_iota(jnp.int32, (h, h * width), 1)
    row = lax.broadcasted_iota(jnp.int32, (h, h * width), 0)
    e = jnp.where(lane // width == row, 1.0, 0.0).astype(F32)
    return mmh(jnp.broadcast_to(p, (8, h)), e)[0:1]


def _group_apply(x, width, fn):
    n = x.shape[1] // width
    return jnp.concatenate([fn(x[:, g * width:(g + 1) * width]) for g in range(n)], axis=1)


def _rms(x):
    return x * lax.rsqrt(jnp.mean(x * x, axis=-1, keepdims=True) + NORM_EPS)


@jax.custom_vjp
def unit_lower_inv(ls):
    n = ls[0].shape[0]
    eye = jnp.where(_masks(n)[0] & jnp.logical_not(_masks(n)[1]), 1.0, 0.0).astype(F32)
    ps = [-l for l in ls]
    xs = [eye + p for p in ps]
    for _ in range(int(math.log2(n)) - 1):
        ps = [mmh(p, p) for p in ps]
        xs = [x + mmh(x, p) for x, p in zip(xs, ps)]
    return tuple(xs)


def _uli_fwd(ls):
    xs = unit_lower_inv(ls)
    return xs, xs


def _uli_bwd(xs, gs):
    ts = [_dg(x, g, 0, 0, HI) for x, g in zip(xs, gs)]
    return (tuple(-_dg(t, x, 1, 1, HI) for t, x in zip(ts, xs)),)


unit_lower_inv.defvjp(_uli_fwd, _uli_bwd)


def scalar_decay_chunk(r_n, n, p, with_dt, rev):
    def fn(s, q, k, v, la, *rest):
        c = q.shape[0]
        incl, _ = _masks(c, rev)
        tril = incl.astype(F32)
        cum = mmh(tril, la)
        cum_t = _dg(la, tril, 0, 1, HI)
        tot = cum[0:1, :] if rev else cum[c - 1:c, :]
        scores = mmb(q, k, 1, 1)
        ys, ss = [], []
        for r in range(r_n):
            cr = cum[:, r:r + 1]
            dec = jnp.exp(jnp.where(incl, cr - cum_t[r:r + 1, :], -jnp.inf))
            vr = v[:, r * p:(r + 1) * p]
            if with_dt:
                vr = vr * rest[0][:, r:r + 1]
            sr = s[r * n:(r + 1) * n, :]
            ys.append(mmb(scores * dec, vr, 1, 0) + mmb(q, sr, 1, 0) * jnp.exp(cr))
            ss.append(jnp.exp(tot[:, r:r + 1]) * sr + mmb(k, jnp.exp(tot[:, r:r + 1] - cr) * vr, 0, 0))
        return jnp.concatenate(ss, axis=0), jnp.concatenate(ys, axis=1)

    return fn


def vector_decay_chunk(rev):
    return functools.partial(_vector_decay_chunk, rev)


def _vector_decay_chunk(rev, s, q, k, v, lf):
    c, kd = q.shape
    incl, _ = _masks(c, rev)
    cum = mmh(incl.astype(F32), lf)
    tot = cum[0:1, :] if rev else cum[c - 1:c, :]
    tot_col = _dg(lf, jnp.ones((c, 1), F32), 0, 0, HI)
    y_inter = mmb(q * jnp.exp(cum), s, 1, 0)
    s_new = jnp.exp(tot_col) * s + mmb(k * jnp.exp(tot - cum), v, 0, 0)
    i3 = lax.broadcasted_iota(jnp.int32, (SUB, SUB, 1), 0)
    j3 = lax.broadcasted_iota(jnp.int32, (SUB, SUB, 1), 1)
    ys = []
    for b in range(c // SUB):
        lo, hi = b * SUB, (b + 1) * SUB
        qb, kb, cb, vb = q[lo:hi], k[lo:hi], cum[lo:hi], v[lo:hi]
        dec = jnp.exp(jnp.where((i3 <= j3) if rev else (i3 >= j3), cb[:, None, :] - cb[None, :, :], -jnp.inf))
        a_diag = jnp.sum(qb[:, None, :] * kb[None, :, :] * dec, axis=-1)
        yb = mmb(a_diag, vb, 1, 0)
        if rev and hi < c:
            base = cum[hi:hi + 1, :]
            a_off = mmb(qb * jnp.exp(cb - base), k[hi:] * jnp.exp(base - cum[hi:]), 1, 1)
            yb = yb + mmb(a_off, v[hi:], 1, 0)
        if not rev and b > 0:
            base = cum[lo - 1:lo, :]
            a_off = mmb(qb * jnp.exp(cb - base), k[:lo] * jnp.exp(base - cum[:lo]), 1, 1)
            yb = yb + mmb(a_off, v[:lo], 1, 0)
        ys.append(yb)
    return s_new, y_inter + jnp.concatenate(ys, axis=0)


def delta_chunk(r_n, kd, vd, rev):
    def fn(s_list, ins_list):
        c = ins_list[0][0].shape[0]
        incl, strict = _masks(c, rev)
        tril = incl.astype(F32)
        heads = [(g, r) for g in range(len(ins_list)) for r in range(r_n)]
        q = [ins[0] for ins in ins_list]
        k = [ins[1] for ins in ins_list]
        cum = [mmh(tril, ins[4]) for ins in ins_list]
        cum_t = [_dg(ins[4], tril, 0, 1, HI) for ins in ins_list]
        kk = [mmb(a, a, 1, 1) for a in k]
        qk = [mmb(a, b, 1, 1) for a, b in zip(q, k)]
        cr = [cum[g][:, r:r + 1] for g, r in heads]
        br = [ins_list[g][3][:, r:r + 1] for g, r in heads]
        seg = [c_ - cum_t[g][r:r + 1, :] for c_, (g, r) in zip(cr, heads)]
        tot = [(cum[g][0:1, :] if rev else cum[g][c - 1:c, :])[:, r:r + 1] for g, r in heads]
        vr = [ins_list[g][2][:, r * vd:(r + 1) * vd] for g, r in heads]
        sr = [s_list[g][r * kd:(r + 1) * kd, :] for g, r in heads]
        xs = unit_lower_inv(tuple(b * kk[g] * jnp.exp(jnp.where(strict, sg, -jnp.inf))
                                  for b, sg, (g, r) in zip(br, seg, heads)))
        u = [mmh(x, v_ * b) for x, v_, b in zip(xs, vr, br)]
        w = [mmh(x, k[g] * (b * jnp.exp(c_))) for x, b, c_, (g, r) in zip(xs, br, cr, heads)]
        ws = [mmb(w_, s_, 1, 0) for w_, s_ in zip(w, sr)]
        v_new = [u_ - a for u_, a in zip(u, ws)]
        ya = [mmb(qk[g] * jnp.exp(jnp.where(incl, sg, -jnp.inf)), vn, 1, 0) for sg, vn, (g, r) in zip(seg, v_new, heads)]
        yb = [mmb(q[g], s_, 1, 0) * jnp.exp(c_) for s_, c_, (g, r) in zip(sr, cr, heads)]
        sn = [jnp.exp(t_) * s_ + mmb(k[g], jnp.exp(t_ - c_) * vn, 0, 0)
              for t_, s_, c_, vn, (g, r) in zip(tot, sr, cr, v_new, heads)]
        out = []
        for g in range(len(ins_list)):
            idx = [i for i, (g2, _) in enumerate(heads) if g2 == g]
            out.append((jnp.concatenate([sn[i] for i in idx], axis=0), jnp.concatenate([ya[i] + yb[i] for i in idx], axis=1)))
        return out

    return fn


def f_adaln(x, g, sh, sc):
    return ((_rms(x) * g) * (1.0 + sc) + sh,)


def f_resid(x, y, gate):
    return (x + gate * y,)


def f_relu2(h):
    return (jnp.square(jnp.maximum(h, 0.0)),)


def f_conv_bias(up, u, un, w0, w1, w2, b):
    return (_silu(w0 * up + w1 * u + w2 * un + b),)


def f_conv(up, u, un, w0, w1, w2):
    return (_silu(w0 * up + w1 * u + w2 * un),)


def f_dt(raw, bias, a_log):
    dt = _softplus(raw + bias)
    return dt, -jnp.exp(a_log) * dt


def f_ssd_post(yf, yb, xs, z, d_skip, ng):
    y = (yf + yb + _expand_heads(d_skip, SSD_HEAD_DIM) * xs) * _silu(z)
    return (_group_apply(y, y.shape[1] // SSD_GROUPS, _rms) * ng,)


def f_rope(scale):
    def f(q, qsw, cos, sin):
        return ((q * cos + qsw * sin) * scale,)
    return f


def f_ret_post(yf, yb, g):
    def ln(y):
        mu = jnp.mean(y, axis=-1, keepdims=True)
        var = jnp.mean(jnp.square(y - mu), axis=-1, keepdims=True)
        return (y - mu) * lax.rsqrt(var + NORM_EPS)
    y = yf + yb
    return (_group_apply(y, y.shape[1] // RET_HEADS, ln) * _silu(g),)


def f_lower_bound(layer):
    def f(logits):
        m = jnp.max(logits, axis=0, keepdims=True)
        e = jnp.exp(logits - m)
        p = e / jnp.sum(e, axis=0, keepdims=True)
        lb = p[1:2]
        for j in range(2, layer + 1):
            lb = lb + p[j:j + 1]
        return (jnp.broadcast_to(lb, logits.shape),)
    return f


def f_hgrn_gates(ff, fb, lb):
    def gates(f):
        log_sig = jnp.minimum(f, 0.0) - jnp.log(1.0 + jnp.exp(-jnp.abs(f)))
        a, b = jnp.log(lb), jnp.log(1.0 - lb) + log_sig
        m = jnp.maximum(a, b)
        return m + jnp.log(jnp.exp(a - m) + jnp.exp(b - m)), (1.0 - lb) * jax.nn.sigmoid(-f)
    lf_f, k_f = gates(ff)
    lf_b, k_b = gates(fb)
    return lf_f, k_f, lf_b, k_b


def f_hgrn_post(yf, yb, g, ng):
    return (_group_apply(yf + yb, HEAD128, _rms) * ng * _silu(g),)


def f_gdn_qk(q, k):
    def l2(x):
        return x * lax.rsqrt(jnp.sum(x * x, axis=-1, keepdims=True) + 1e-6)
    return _group_apply(q, HEAD128, l2) * HEAD128 ** -0.5, _group_apply(k, HEAD128, l2)


def f_gdn_gates(bt, a, dt_bias, a_log):
    return jax.nn.sigmoid(bt), -jnp.exp(a_log) * _softplus(a + dt_bias)


def f_gdn_post(yf, yb, z, ng):
    n = yf.shape[1] // HEAD128
    return (_group_apply(yf + yb, HEAD128, _rms) * jnp.concatenate([ng] * n, axis=1) * _silu(z),)


def f_loss(x, tgt, fg):
    err = jnp.square(_rms(x) * fg - tgt)
    return (jnp.broadcast_to(jnp.mean(err, axis=-1, keepdims=True), (x.shape[0], LANES)),)


def _par(v):
    return v.reshape(1, 1, -1)


def _shift(u, lc):
    z = jnp.zeros((1, u.shape[1]), u.dtype)
    prev = jnp.concatenate([z, u[:lc - 1], z, u[lc:-1]], axis=0)
    nxt = jnp.concatenate([u[1:lc], z, u[lc + 1:], z], axis=0)
    return prev, nxt


def _grp(a, g_n):
    t = a.shape[0]
    return a.reshape(t, g_n, -1).transpose(1, 0, 2)


def ssd_mixer(h, p, lc, start, carry):
    d = h.shape[1]
    di = 2 * d
    gn = SSD_GROUPS * SSD_STATE
    heads = di // SSD_HEAD_DIM
    r_n = heads // SSD_GROUPS
    w_z, w_x, w_dt = split_cols(p["ssd_w_in"], [di, 2 * di + 2 * gn])
    xbc, got0 = cmm(h, w_x, "ssd_in_x", carry[0])
    z, got1 = cmm(h, w_z, "ssd_in_z", carry[1])
    dtr = pmatmul(h, w_dt, "ssd_in_dt")
    up, un = _shift(xbc, lc)
    cw = p["ssd_conv_w"]
    (xbc,) = rowop(f_conv_bias, "ssd_conv", [up, xbc, un], [_par(cw[0]), _par(cw[1]), _par(cw[2]), _par(p["ssd_conv_b"])],
                   [xbc.shape[1]], 128)
    xs, bm, cm = xbc[:, :di], xbc[:, di:di + gn], xbc[:, di + gn:]
    dt, la = rowop(f_dt, "ssd_dt", [dtr], [_par(p["ssd_dt_bias"]), _par(p["ssd_a_log"])], [2 * heads, 2 * heads], 256)

    def scan(tag, rev, grps, cr):
        return _with_got(scanop(scalar_decay_chunk(r_n, SSD_STATE, SSD_HEAD_DIM, True, rev), "ssd_scan_" + tag,
                                [cm, bm, xs], [SSD_STATE, SSD_STATE, r_n * SSD_HEAD_DIM], grps,
                                (r_n * SSD_STATE, SSD_HEAD_DIM), r_n * SSD_HEAD_DIM, rev, lc, carry=cr), cr)

    yf, got2 = scan("f", False, [_grp(la[:, :heads], SSD_GROUPS), _grp(dt[:, :heads], SSD_GROUPS)], carry[2])
    yb, got3 = scan("b", True, [_grp(la[:, heads:], SSD_GROUPS), _grp(dt[:, heads:], SSD_GROUPS)], carry[3])
    (yn,) = rowop(f_ssd_post, "ssd_post", [yf[start:], yb[start:], xs[start:], z[start:]],
                  [_par(p["ssd_d"]), _par(p["ssd_norm_g"])], [di], 128)
    return pmatmul(yn, p["ssd_w_out"], "ssd_out"), got0 + got1 + got2 + got3


def _rope_tables(n_lat, lc, d, heads):
    qk = d // heads
    half = qk // 2
    quarter = half // 2
    pos = jnp.arange(n_lat)
    inv_freq = ROPE_BASE ** (-jnp.arange(0, half, 2, dtype=F32) / half)

    def tab(p):
        ang = p.astype(F32)[:, None] * inv_freq
        return jnp.cos(ang), jnp.sin(ang)

    cr, sr = tab(pos // GRID_W)
    cc, sc = tab(pos % GRID_W)
    cos = jnp.concatenate([cr, cr, cc, cc], axis=1)
    sin = jnp.concatenate([-sr, sr, -sc, sc], axis=1)
    cos = jnp.concatenate([jnp.ones((lc, qk), F32), cos], axis=0)
    sin = jnp.concatenate([jnp.zeros((lc, qk), F32), sin], axis=0)
    return jnp.tile(cos, (1, heads)), jnp.tile(sin, (1, heads)), quarter


def retention_mixer(h, p, lc, start, carry):
    t, d = h.shape
    dv = 2 * d
    qk = d // RET_HEADS
    w_q, w_k, w_v, w_g = split_cols(p["ret_w_in"], [d, 2 * d, 2 * d + dv])
    v, got0 = cmm(h, w_v, "ret_in_v", carry[0])
    g, got1 = cmm(h, w_g, "ret_in_g", carry[1])
    q, k = pmatmul(h, w_q, "ret_in_q"), pmatmul(h, w_k, "ret_in_k")
    cos, sin, quarter = _rope_tables(t - lc, lc, d, RET_HEADS)
    cos, sin = lax.stop_gradient(cos), lax.stop_gradient(sin)

    def swap(a):
        return a.reshape(t, -1, 2, quarter)[:, :, ::-1, :].reshape(t, d)

    (q,) = rowop(f_rope(1.0), "ret_rope_q", [q, swap(q), cos, sin], [], [d], 256)
    (k,) = rowop(f_rope(qk ** -0.5), "ret_rope_k", [k, swap(k), cos, sin], [], [d], 256)

    def scan(tag, rev, grps, cr):
        return _with_got(scanop(scalar_decay_chunk(1, qk, dv // RET_HEADS, False, rev), "ret_scan_" + tag, [q, k, v],
                                [qk, qk, dv // RET_HEADS], grps, (qk, dv // RET_HEADS), dv // RET_HEADS, rev, lc,
                                GROUPS_PER_STEP, carry=cr), cr)

    ld = p["ret_log_decay"]
    la_f = jnp.broadcast_to(ld[0][:, None, None], (RET_HEADS, t, 1))
    la_b = jnp.broadcast_to(ld[1][:, None, None], (RET_HEADS, t, 1))
    (yf, got2), (yb, got3) = scan("f", False, [la_f], carry[2]), scan("b", True, [la_b], carry[3])
    (yn,) = rowop(f_ret_post, "ret_post", [yf[start:], yb[start:], g[start:]], [], [dv], 128)
    return pmatmul(yn, p["ret_w_out"], "ret_out"), got0 + got1 + got2 + got3


def hgrn2_mixer(h, p, lc, start, layer, carry):
    t, d = h.shape
    w_q, w_ff, w_fb, w_i, w_g = split_cols(p["hgrn_w_in"], [d, 2 * d, 3 * d, 4 * d])
    q, got0 = cmm(h, w_q, "hgrn_in_q", carry[0])
    f_f, got1 = cmm(h, w_ff, "hgrn_in_ff", carry[1])
    f_b = pmatmul(h, w_fb, "hgrn_in_fb")
    i, g = pmatmul(h, w_i, "hgrn_in_i"), pmatmul(h, w_g, "hgrn_in_g")
    (lb,) = rowop(f_lower_bound(layer), "hgrn_lb", [p["hgrn_lb_logits"]], [], [d], DEPTH)
    lf_f, k_f, lf_b, k_b = rowop(f_hgrn_gates, "hgrn_gates", [f_f, f_b], [lb[0:1][None]], [d] * 4, 256)

    def scan(tag, rev, cols, cr):
        return _with_got(scanop(vector_decay_chunk(rev), "hgrn_scan_" + tag, cols, [HEAD128] * 4, [], (HEAD128, HEAD128),
                                HEAD128, rev, lc, GROUPS_PER_STEP, carry=cr), cr)

    yf, got2 = scan("f", False, [q, k_f, i, lf_f], carry[2])
    yb, got3 = scan("b", True, [q, k_b, i, lf_b], carry[3])
    (yn,) = rowop(f_hgrn_post, "hgrn_post", [yf[start:], yb[start:], g[start:]], [_par(p["hgrn_norm_g"])], [d], 256)
    return pmatmul(yn, p["hgrn_w_out"], "hgrn_out"), got0 + got1 + got2 + got3


def gdn_mixer(h, p, lc, start):
    t, d = h.shape
    dk, dv = d, 2 * d
    kh = d // HEAD128
    r_n = 2
    cc = 2 * dk + dv
    nb = 2 * kh * r_n
    w_qkv, w_z, w_bt, w_a = split_cols(p["gdn_w_in"], [cc, cc + dv, cc + dv + nb])
    qkv, z = pmatmul(h, w_qkv, "gdn_in_qkv"), pmatmul(h, w_z, "gdn_in_z")
    bt, a = pmatmul(h, w_bt, "gdn_in_bt"), pmatmul(h, w_a, "gdn_in_a")
    up, un = _shift(qkv, lc)
    cw = p["gdn_conv_w"]
    (qkv,) = rowop(f_conv, "gdn_conv", [up, qkv, un], [_par(cw[0]), _par(cw[1]), _par(cw[2])], [cc], 128)
    q, k, v = qkv[:, :dk], qkv[:, dk:2 * dk], qkv[:, 2 * dk:]
    q, k = rowop(f_gdn_qk, "gdn_qk", [q, k], [], [dk, dk], 256)
    beta, la = rowop(f_gdn_gates, "gdn_gates", [bt, a], [_par(p["gdn_dt_bias"]), _par(p["gdn_a_log"])], [nb, nb], 256)

    def scan(tag, rev, grps):
        return scanop(delta_chunk(r_n, HEAD128, HEAD128, rev), "gdn_scan_" + tag, [q, k, v],
                      [HEAD128, HEAD128, r_n * HEAD128], grps, (r_n * HEAD128, HEAD128), r_n * HEAD128, rev, lc,
                      GROUPS_PER_STEP, True)

    half = nb // 2
    yf = scan("f", False, [_grp(beta[:, :half], kh), _grp(la[:, :half], kh)])
    yb = scan("b", True, [_grp(beta[:, half:], kh), _grp(la[:, half:], kh)])
    (yn,) = rowop(f_gdn_post, "gdn_post", [yf[start:], yb[start:], z[start:]], [_par(p["gdn_norm_g"])], [dv], 128)
    return pmatmul(yn, p["gdn_w_out"], "gdn_out")


@jax.custom_vjp
def tap(w, probe):
    return w


tap.defvjp(lambda w, probe: (w, None), lambda _, g: (jnp.zeros_like(g), g))


def split_cols(w, cuts):
    bounds = list(zip([0] + list(cuts), list(cuts) + [w.shape[1]]))

    @jax.custom_vjp
    def f(w):
        return tuple(w[:, a:b] for a, b in bounds)

    f.defvjp(lambda w: (f(w), None), lambda _, gs: (jnp.concatenate(gs, axis=1),))
    return f(w)


def _with_got(res, carry):
    return res if carry else (res, [])


def cmm(a, w, name, carry):
    if carry:
        return pmatmul(a, w, name, carry)
    return pmatmul(a, w, name), []


MIXER_W = [("ssd_w_in", "ssd_w_out"), ("ret_w_in", "ret_w_out"), ("hgrn_w_in", "hgrn_w_out"), ("gdn_w_in", "gdn_w_out")]


def local_loss(x, ctx, mod2, tgt, p, shards, shapes):
    lc, d = ctx.shape
    p = dict(p)
    xc = jnp.concatenate([ctx, x], axis=0)
    for i in range(DEPTH):
        keep = i < DEPTH - 1
        start = 0 if keep else lc
        carry, out_carry = [None] * 4, None
        if keep:
            w_in_n, w_out_n = MIXER_W[i + 1]
            half = shards[w_in_n].shape[0] // 2
            carry = [[shards[w_in_n][:half]], [shards[w_in_n][half:]], [shards["mlp_w1_%d" % (i + 1)]],
                     [shards["mlp_w2_%d" % (i + 1)]]]
            out_carry = [shards[w_out_n]]
        sh1, sc1, g1, sh2, sc2, g2 = (mod2[i][:, j * d:(j + 1) * d][:, None, :] for j in range(6))
        (h,) = rowop(f_adaln, "adaln_a%d" % i, [xc], [_par(p["norm_g"][i, 0]), sh1, sc1], [d], 256, lc)
        if i % 4 == 0:
            y, got = ssd_mixer(h, p, lc, start, carry)
        elif i % 4 == 1:
            y, got = retention_mixer(h, p, lc, start, carry)
        elif i % 4 == 2:
            y, got = hgrn2_mixer(h, p, lc, start, i, carry)
        else:
            y, got = gdn_mixer(h, p, lc, start), []
        if not keep:
            xc, g1, sh2, sc2, g2, lc = xc[lc:], g1[1:], sh2[1:], sc2[1:], g2[1:], 0
        (xc,) = rowop(f_resid, "resid_a%d" % i, [xc, y], [g1], [d], 256, lc)
        (h2,) = rowop(f_adaln, "adaln_b%d" % i, [xc], [_par(p["norm_g"][i, 1]), sh2, sc2], [d], 256, lc)
        a, got_out = cmm(h2, p["mlp_w1_%d" % i], "mlp1_%d" % i, out_carry)
        (a,) = rowop(f_relu2, "relu2_%d" % i, [a], [], [a.shape[1]], 128)
        m = pmatmul(a, p["mlp_w2_%d" % i], "mlp2_%d" % i)
        (xc,) = rowop(f_resid, "resid_b%d" % i, [xc, m], [g2], [d], 256, lc)
        if keep:
            arrived = [(w_in_n, jnp.concatenate(got[:2], axis=1)), ("mlp_w1_%d" % (i + 1), got[2]),
                       ("mlp_w2_%d" % (i + 1), got[3]), (w_out_n, got_out[0])]
            for n, g8 in arrived:
                shp, axis = shapes[n]
                p[n] = tap(_to_full(g8.reshape((N_DEV,) + shp), axis)[0], p[n])
    (rows,) = rowop(f_loss, "loss", [xc, tgt], [_par(p["final_g"])], [LANES], 256)
    return 0.5 * jnp.sum(rows[:, 0])


def exchange(x, name):
    def body(x_ref, o_ref, send_sems, recv_sems, local_sem):
        mx, my, mc = lax.axis_index("x"), lax.axis_index("y"), lax.axis_index("c")
        me = 4 * mx + 2 * my + mc
        copies = []
        for k in range(1, N_DEV):
            px = 1 - mx if (k >> 2) & 1 else mx
            py = 1 - my if (k >> 1) & 1 else my
            pc = 1 - mc if k & 1 else mc
            cp = pltpu.make_async_remote_copy(src_ref=x_ref, dst_ref=o_ref.at[me], send_sem=send_sems.at[k - 1],
                                              recv_sem=recv_sems.at[k - 1], device_id=(px, py, pc), device_id_type=MESH)
            cp.start()
            copies.append(cp)
        mine = pltpu.make_async_copy(x_ref, o_ref.at[me], local_sem)
        mine.start()
        for cp in copies:
            cp.wait_recv()
        for cp in copies:
            cp.wait_send()
        mine.wait()

    return pl.pallas_call(
        body, out_shape=SDS((N_DEV,) + x.shape, x.dtype),
        in_specs=[pl.BlockSpec(memory_space=pl.ANY)], out_specs=pl.BlockSpec(memory_space=pl.ANY),
        scratch_shapes=[pltpu.SemaphoreType.DMA((N_DEV - 1,)), pltpu.SemaphoreType.DMA((N_DEV - 1,)),
                        pltpu.SemaphoreType.DMA],
        name=name)(x)


def _place():
    mx, my, mc = lax.axis_index("x"), lax.axis_index("y"), lax.axis_index("c")
    chips = [(1 - mx, my), (mx, 1 - my), (1 - mx, 1 - my)]
    return mx, my, mc, chips


def _any_specs(n):
    return [pl.BlockSpec(memory_space=pl.ANY)] * n


def _gather_sems(n):
    return [pltpu.SemaphoreType.DMA((n, 7)), pltpu.SemaphoreType.DMA((n, 7)), pltpu.SemaphoreType.DMA((n,))]


def _gather_copies(x_refs, o_refs, send_sems, recv_sems, local_sems):
    mx, my, mc, chips = _place()
    me, sib = (mx, my, mc), (mx, my, 1 - mc)

    def copy(a, k, block, to, src=None):
        dst = o_refs[a].at[4 * block[0] + 2 * block[1] + block[2]]
        return pltpu.make_async_remote_copy(src_ref=dst if src is None else src, dst_ref=dst, send_sem=send_sems.at[a, k],
                                            recv_sem=recv_sems.at[a, k], device_id=to, device_id_type=MESH)

    n = len(x_refs)
    mine = [pltpu.make_async_copy(x_refs[a], o_refs[a].at[4 * mx + 2 * my + mc], local_sems.at[a]) for a in range(n)]
    own = [copy(a, 1 + j, me, (*chip, mc), src=x_refs[a]) for a in range(n) for j, chip in enumerate(chips)]
    own += [copy(a, 0, me, sib, src=x_refs[a]) for a in range(n)]
    return mc, chips, me, sib, copy, mine, own


def _gather_start(x_refs, o_refs, send_sems, recv_sems, local_sems):
    *_, mine, own = _gather_copies(x_refs, o_refs, send_sems, recv_sems, local_sems)
    for cp in mine + own:
        cp.start()


def _gather_finish(x_refs, o_refs, send_sems, recv_sems, local_sems):
    mc, chips, me, sib, copy, mine, own = _gather_copies(x_refs, o_refs, send_sems, recv_sems, local_sems)
    n = len(x_refs)
    passed = []
    for a in range(n):
        for j, chip in enumerate(chips):
            copy(a, 1 + j, (*chip, mc), me).wait_recv()
            fwd = copy(a, 4 + j, (*chip, mc), sib)
            fwd.start()
            passed.append(fwd)
    for a in range(n):
        copy(a, 0, sib, me).wait_recv()
        for j, chip in enumerate(chips):
            copy(a, 4 + j, (*chip, 1 - mc), me).wait_recv()
    for cp in own + passed:
        cp.wait_send()
    for cp in mine:
        cp.wait()


def gather_two_level(arrs, name):
    n = len(arrs)

    def body(*refs):
        _gather_start(refs[:n], refs[n:2 * n], *refs[2 * n:])
        _gather_finish(refs[:n], refs[n:2 * n], *refs[2 * n:])

    return pl.pallas_call(
        body, out_shape=[SDS((N_DEV,) + a.shape, a.dtype) for a in arrs], in_specs=_any_specs(n), out_specs=_any_specs(n),
        scratch_shapes=_gather_sems(n), name=name)(*arrs)


def scatter_sibling(parts, name):
    n = len(parts)

    def body(*refs):
        x_refs, o_refs = refs[:n], refs[n:2 * n]
        send_sems, recv_sems = refs[2 * n:]
        mx, my, mc, _ = _place()
        copies = []
        for a in range(n):
            for b in range(4):
                cp = pltpu.make_async_remote_copy(src_ref=x_refs[a].at[b, 1 - mc], dst_ref=o_refs[a].at[b],
                                                  send_sem=send_sems.at[a, b], recv_sem=recv_sems.at[a, b],
                                                  device_id=(mx, my, 1 - mc), device_id_type=MESH)
                cp.start()
                copies.append(cp)
        for cp in copies:
            cp.wait_recv()
        for cp in copies:
            cp.wait_send()

    return pl.pallas_call(
        body, out_shape=[SDS((4,) + p.shape[2:], p.dtype) for p in parts], in_specs=_any_specs(n), out_specs=_any_specs(n),
        scratch_shapes=[pltpu.SemaphoreType.DMA((n, 4)), pltpu.SemaphoreType.DMA((n, 4))], name=name)(*parts)


def pair_sum(mine, theirs, name):
    _, rows, cols = mine.shape
    tr = _pick(rows, max(8, (1 << 19) // cols), 8)

    def body(a_ref, b_ref, o_ref):
        o_ref[...] = (a_ref[...].astype(F32) + b_ref[...].astype(F32)).astype(o_ref.dtype)

    spec = pl.BlockSpec((None, tr, cols), lambda b, i: (b, i, 0))
    return pl.pallas_call(body, grid=(4, rows // tr), in_specs=[spec, spec], out_specs=spec,
                          out_shape=SDS(mine.shape, mine.dtype), compiler_params=_cp("parallel", "parallel"), name=name)(mine, theirs)


def scatter_chips(sums, name):
    n = len(sums)

    def body(*refs):
        x_refs, o_refs = refs[:n], refs[n:2 * n]
        send_sems, recv_sems, local_sems = refs[2 * n:]
        mx, my, mc, chips = _place()
        my_chip = 2 * mx + my
        mine = [pltpu.make_async_copy(x_refs[a].at[my_chip], o_refs[a].at[my_chip], local_sems.at[a]) for a in range(n)]
        for cp in mine:
            cp.start()
        copies = []
        for a in range(n):
            for j, (px, py) in enumerate(chips):
                cp = pltpu.make_async_remote_copy(src_ref=x_refs[a].at[2 * px + py], dst_ref=o_refs[a].at[my_chip],
                                                  send_sem=send_sems.at[a, j], recv_sem=recv_sems.at[a, j],
                                                  device_id=(px, py, mc), device_id_type=MESH)
                cp.start()
                copies.append(cp)
        for cp in copies:
            cp.wait_recv()
        for cp in copies:
            cp.wait_send()
        for cp in mine:
            cp.wait()

    return pl.pallas_call(
        body, out_shape=[SDS(p.shape, p.dtype) for p in sums], in_specs=_any_specs(n), out_specs=_any_specs(n),
        scratch_shapes=[pltpu.SemaphoreType.DMA((n, 3)), pltpu.SemaphoreType.DMA((n, 3)), pltpu.SemaphoreType.DMA((n,))],
        name=name)(*sums)


def sum_parts(parts, name):
    n_p, rows, _ = parts.shape
    tr = _pick(rows, 1024, 8)

    def body(p_ref, o_ref):
        acc = p_ref[0].astype(F32)
        for j in range(1, n_p):
            acc = acc + p_ref[j].astype(F32)
        o_ref[...] = acc

    return pl.pallas_call(body, grid=(rows // tr,), in_specs=[pl.BlockSpec((n_p, tr, LANES), lambda i: (0, i, 0))],
                          out_specs=pl.BlockSpec((tr, LANES), lambda i: (i, 0)), out_shape=SDS((rows, LANES), F32),
                          compiler_params=_cp("parallel"), name=name)(parts)


def adamw(w, parts, m, v, name):
    rows, cols = w.shape
    n_p = parts.shape[0]
    tr = _pick(rows, max(8, (1 << 18) // cols), 8)
    c1 = 1.0 - ADAM_B1 ** ADAM_STEP
    c2 = 1.0 - ADAM_B2 ** ADAM_STEP

    def body(w_ref, p_ref, m_ref, v_ref, g_out, d_out, m_out, v_out):
        g = p_ref[0].astype(F32)
        for j in range(1, n_p):
            g = g + p_ref[j].astype(F32)
        m_new = ADAM_B1 * m_ref[...] + (1.0 - ADAM_B1) * g
        v_new = ADAM_B2 * v_ref[...] + (1.0 - ADAM_B2) * jnp.square(g)
        g_out[...] = g
        m_out[...] = m_new
        v_out[...] = v_new
        d_out[...] = -ADAM_LR * ((m_new / c1) / (jnp.sqrt(v_new / c2) + ADAM_EPS) + ADAM_WD * w_ref[...])

    spec = pl.BlockSpec((tr, cols), lambda i: (i, 0))
    return pl.pallas_call(
        body, grid=(rows // tr,), in_specs=[spec, pl.BlockSpec((n_p, tr, cols), lambda i: (0, i, 0)), spec, spec],
        out_specs=[spec] * 4, out_shape=[SDS((rows, cols), F32)] * 4, compiler_params=_cp("parallel"), name=name)(w, parts, m, v)


FWD_NAMES = ["x", "c", "ctx", "c_ctx", "ada_w", "ada_b", "norm_g", "mlp_w1", "mlp_w2", "final_g", "ssd_w_in", "ssd_conv_w",
             "ssd_conv_b", "ssd_dt_bias", "ssd_a_log", "ssd_d", "ssd_norm_g", "ssd_w_out", "ret_w_in", "ret_log_decay",
             "ret_w_out", "hgrn_w_in", "hgrn_lb_logits", "hgrn_norm_g", "hgrn_w_out", "gdn_w_in", "gdn_conv_w", "gdn_dt_bias",
             "gdn_a_log", "gdn_norm_g", "gdn_w_out"]
WEIGHTS = FWD_NAMES[3:]
BIG = {"mlp_w1": 2, "mlp_w2": 1, "ssd_w_in": 2, "ssd_w_out": 1, "ret_w_in": 2, "ret_w_out": 1, "hgrn_w_in": 2,
       "hgrn_w_out": 1, "gdn_w_in": 2, "gdn_w_out": 1}
SMALL_SHARDED = ["norm_g", "ssd_conv_w", "gdn_conv_w", "hgrn_norm_g"]
SMALL = [n for n in WEIGHTS if n not in BIG and n != "ada_w"]


def _to_full(g8, axis):
    _, l, a, b = g8.shape
    if axis == 1:
        return g8.transpose(1, 0, 2, 3).reshape(l, N_DEV * a, b)
    return g8.transpose(1, 2, 0, 3).reshape(l, a, N_DEV * b)


def _to_shards(full, axis):
    l, a, b = full.shape
    if axis == 1:
        return full.reshape(l, N_DEV, a // N_DEV, b).transpose(1, 0, 2, 3)
    return full.reshape(l, a, N_DEV, b // N_DEV).transpose(2, 0, 1, 3)


def _pack_small(arrs):
    parts, meta, off = [], [], 0
    for a in arrs:
        n = a.size
        npad = -(-n // LANES) * LANES
        parts.append(jnp.pad(a.reshape(-1).astype(F32), (0, npad - n)))
        meta.append((off, n, a.shape))
        off += npad
    rows = -(-(off // LANES) // 8) * 8
    flat = jnp.concatenate(parts)
    flat = jnp.pad(flat, (0, rows * LANES - off))
    return flat.reshape(rows, LANES), meta


def _unpack_small(buf, meta):
    flat = buf.reshape(buf.shape[:-2] + (-1,))
    return [flat[..., off:off + n].reshape(buf.shape[:-2] + tuple(shape)) for off, n, shape in meta]


def _my_shard(full, me, n_local):
    return lax.dynamic_slice_in_dim(full, me * n_local, n_local, axis=full.ndim - 1)


def kernel(*args):
    n_f = len(FWD_NAMES)
    inp = dict(zip(FWD_NAMES, args[:n_f]))
    tgt = args[n_f][0]
    n_w = len(WEIGHTS)
    mom_m = dict(zip(WEIGHTS, args[n_f + 1:n_f + 1 + n_w]))
    mom_v = dict(zip(WEIGHTS, args[n_f + 1 + n_w:n_f + 1 + 2 * n_w]))
    x, ctx, c = inp["x"][0], inp["ctx"][0], inp["c"]
    d = x.shape[1]
    me = 4 * lax.axis_index("x") + 2 * lax.axis_index("y") + lax.axis_index("c")

    buf, meta = _pack_small([c] + [inp[n] for n in SMALL_SHARDED])
    got = _unpack_small(exchange(buf, "gather_small"), meta)
    c_all = got[0].reshape(N_DEV, d)
    small_full = {}
    for n, g8 in zip(SMALL_SHARDED, got[1:]):
        small_full[n] = jnp.moveaxis(g8, 0, -2).reshape(g8.shape[1:-1] + (N_DEV * g8.shape[-1],))
    cond_in = jnp.concatenate([c_all, inp["c_ctx"][None]], axis=0)
    cond = _silu(cond_in)
    cond16 = jnp.pad(cond, ((0, 16 - cond.shape[0]), (0, 0)))

    ada_w = inp["ada_w"]
    n_ada = ada_w.shape[2]
    mod_loc = jnp.stack([_mm(cond16, ada_w[i], "nn", "ada_fwd")[:N_DEV + 1] for i in range(DEPTH)])
    mbuf, mmeta = _pack_small([mod_loc])
    (mod8,) = _unpack_small(exchange(mbuf, "gather_mod"), mmeta)
    mod_full = mod8.transpose(1, 2, 0, 3).reshape(DEPTH, N_DEV + 1, N_DEV * n_ada) + inp["ada_b"][:, None, :]
    mod2 = jnp.stack([mod_full[:, N_DEV], lax.dynamic_index_in_dim(mod_full, me, axis=1, keepdims=False)], axis=1)

    big = list(BIG)
    shards, shapes = {}, {}
    for n in big:
        w16 = inp[n].astype(BF16)
        for i in range(w16.shape[0]):
            key = "%s_%d" % (n, i) if w16.shape[0] > 1 else n
            shards[key], shapes[key] = w16[i], ((1,) + w16.shape[1:], BIG[n])
    first = ["ssd_w_in", "ssd_w_out", "mlp_w1_0", "mlp_w2_0"]
    p = {}
    for n, g8 in zip(first, gather_two_level([shards[n] for n in first], "gather_weights")):
        p[n] = _to_full(g8.reshape((N_DEV,) + shapes[n][0]), shapes[n][1])[0]
    later = {n: s for n, s in shards.items() if n not in first}
    for n, s in later.items():
        rows, cols = s.shape
        p[n] = jnp.zeros((N_DEV * rows, cols) if shapes[n][1] == 1 else (rows, N_DEV * cols), BF16)
    for n in SMALL:
        if n not in ("c_ctx", "ada_b"):
            p[n] = small_full[n] if n in small_full else inp[n]
    for n in ("ssd_conv_w", "ssd_conv_b", "ssd_dt_bias", "ssd_a_log", "ssd_d", "ssd_norm_g", "ret_log_decay", "hgrn_norm_g",
              "gdn_conv_w", "gdn_dt_bias", "gdn_a_log", "gdn_norm_g"):
        p[n] = p[n][0]

    loss_loc, (g_mod2, g_x, g_p) = jax.value_and_grad(
        lambda mod2_, x_, p_: local_loss(x_, ctx, mod2_, tgt, p_, later, shapes), argnums=(0, 1, 2))(mod2, x, p)

    small_g_names = [n for n in SMALL if n not in ("c_ctx", "ada_b")]
    gbuf, gmeta = _pack_small([loss_loc.reshape(1), g_mod2] + [g_p[n] for n in small_g_names])
    g8 = exchange(gbuf, "gather_small_grads")
    gsum = _unpack_small(sum_parts(g8, "sum_small_grads"), gmeta)
    loss = gsum[0][0]
    g_small = dict(zip(small_g_names, gsum[2:]))
    for n in ("ssd_conv_w", "ssd_conv_b", "ssd_dt_bias", "ssd_a_log", "ssd_d", "ssd_norm_g", "ret_log_decay", "hgrn_norm_g",
              "gdn_conv_w", "gdn_dt_bias", "gdn_a_log", "gdn_norm_g"):
        g_small[n] = g_small[n][None]
    dmod_each = _unpack_small(g8, gmeta)[1]
    dmod9 = jnp.concatenate([dmod_each[:, :, 1].transpose(1, 0, 2), gsum[1][:, 0:1]], axis=1)
    g_small["ada_b"] = gsum[1][:, 0] + gsum[1][:, 1]
    dmod16 = jnp.pad(_my_shard(dmod9, me, n_ada), ((0, 0), (0, 16 - dmod9.shape[1]), (0, 0)))
    g_ada_w = jnp.stack([_mm(cond16, dmod16[i], "tn", "ada_dw") for i in range(DEPTH)])
    dcond_part = _mm(dmod16[0], ada_w[0], "nt", "ada_dx")
    for i in range(1, DEPTH):
        dcond_part = dcond_part + _mm(dmod16[i], ada_w[i], "nt", "ada_dx")
    cbuf, cmeta = _pack_small([dcond_part[N_DEV]])
    (dcond8,) = _unpack_small(sum_parts(exchange(cbuf, "gather_dcond"), "sum_dcond"), cmeta)
    g_small["c_ctx"] = jax.vjp(_silu, inp["c_ctx"])[1](dcond8)[0]

    my_core = lax.axis_index("c")
    blocks = []
    for n in big:
        layers = inp[n].shape[0]
        gf = jnp.stack([g_p["%s_%d" % (n, i)] for i in range(layers)]) if layers > 1 else g_p[n][None]
        blocks.append(_to_shards(gf, BIG[n]).reshape(4, 2, -1, inp[n].shape[2]))
    from_sibling = scatter_sibling(blocks, "scatter_grads_sibling")
    sums = [pair_sum(lax.dynamic_index_in_dim(b, my_core, axis=1, keepdims=False), t, "pair_sum_" + n)
            for n, b, t in zip(big, blocks, from_sibling)]
    parts = scatter_chips(sums, "scatter_grads_chips")

    out = {}

    def view(a):
        return a.reshape(-1, a.shape[-1])

    for n, pr in zip(big, parts):
        res = adamw(view(inp[n]), pr, view(mom_m[n]), view(mom_v[n]), "adamw_" + n)
        out[n] = [r.reshape(inp[n].shape) for r in res]
    res = adamw(view(ada_w), view(g_ada_w)[None], view(mom_m["ada_w"]), view(mom_v["ada_w"]), "adamw_ada_w")
    out["ada_w"] = [r.reshape(ada_w.shape) for r in res]
    for n in SMALL_SHARDED:
        g_small[n] = _my_shard(g_small[n], me, inp[n].shape[-1])
    wb, wmeta = _pack_small([inp[n] for n in SMALL])
    gb, _ = _pack_small([g_small[n] for n in SMALL])
    mb, _ = _pack_small([mom_m[n] for n in SMALL])
    vb, _ = _pack_small([mom_v[n] for n in SMALL])
    res = [_unpack_small(r, wmeta) for r in adamw(wb, gb[None], mb, vb, "adamw_small")]
    for j, n in enumerate(SMALL):
        out[n] = [r[j] for r in res]

    outs = [loss, g_x[None]]
    for k in range(4):
        outs += [out[n][k] for n in WEIGHTS]
    return tuple(outs)
```

```python
import functools
import math

import jax
import jax.numpy as jnp
from jax import lax
from jax.experimental import pallas as pl
from jax.experimental.pallas import tpu as pltpu

F32 = jnp.float32
BF16 = jnp.bfloat16
HI = lax.Precision.HIGHEST

N_DEV = 8
CHUNK = 64
SUB = 16
GROUPS_PER_STEP = 2
GDN_GROUPS_PER_STEP = 4
GRID_W = 64
ROPE_BASE = 10000.0
NORM_EPS = 1e-6
DEPTH = 4
SSD_GROUPS = 8
SSD_STATE = 128
SSD_HEAD_DIM = 64
RET_HEADS = 8
HEAD128 = 128
ADAM_LR, ADAM_B1, ADAM_B2, ADAM_EPS, ADAM_WD, ADAM_STEP = 0.001, 0.9, 0.999, 1e-08, 0.01, 10

V7X_VMEM_BYTES = 64 * 1024 * 1024
VMEM_LIMIT = (V7X_VMEM_BYTES * 3) // 4
ROW_PIPELINE_BYTES = V7X_VMEM_BYTES // 4
LANES = 128
MESH = pl.DeviceIdType.MESH

SDS = jax.ShapeDtypeStruct


def _cp(*sem):
    return pltpu.CompilerParams(dimension_semantics=tuple(sem), vmem_limit_bytes=VMEM_LIMIT)


def _pick(n, cap, quantum=LANES):
    best = None
    d = quantum
    while d <= min(n, cap):
        if n % d == 0:
            best = d
        d += quantum
    return n if best is None else best


def _mm(a, b, mode, name, out_dtype=F32, carry=()):
    if mode == "nn":
        (m, k), n = a.shape, b.shape[1]
    elif mode == "nt":
        (m, k), n = a.shape, b.shape[0]
    else:
        (k, m), n = a.shape, b.shape[1]
    tm, tn, tk = _pick(m, 1024), _pick(n, 1280), _pick(k, 2048)
    nk = k // tk
    if mode == "nn":
        a_spec = pl.BlockSpec((tm, tk), lambda i, j, kk: (i, kk))
        b_spec = pl.BlockSpec((tk, tn), lambda i, j, kk: (kk, j))
        dims = (((1,), (0,)), ((), ()))
    elif mode == "nt":
        a_spec = pl.BlockSpec((tm, tk), lambda i, j, kk: (i, kk))
        b_spec = pl.BlockSpec((tn, tk), lambda i, j, kk: (j, kk))
        dims = (((1,), (1,)), ((), ()))
    else:
        a_spec = pl.BlockSpec((tk, tm), lambda i, j, kk: (kk, i))
        b_spec = pl.BlockSpec((tk, tn), lambda i, j, kk: (kk, j))
        dims = (((0,), (0,)), ((), ()))

    n_x = len(carry)
    ni, nj = m // tm, n // tn

    def body(*refs):
        a_ref, b_ref = refs[:2]
        x_refs = refs[2:2 + n_x]
        o_ref = refs[2 + n_x]
        g_refs = refs[3 + n_x:3 + 2 * n_x]
        acc_ref = refs[3 + 2 * n_x]
        sems = refs[4 + 2 * n_x:]
        i, j, kk = pl.program_id(0), pl.program_id(1), pl.program_id(2)

        if n_x:
            @pl.when((i == 0) & (j == 0) & (kk == 0))
            def _():
                _gather_start(x_refs, g_refs, *sems)

        @pl.when(kk == 0)
        def _():
            acc_ref[...] = jnp.zeros_like(acc_ref)

        acc_ref[...] += lax.dot_general(a_ref[...].astype(BF16), b_ref[...].astype(BF16), dims,
                                        preferred_element_type=F32)

        @pl.when(kk == nk - 1)
        def _():
            o_ref[...] = acc_ref[...].astype(o_ref.dtype)

        if n_x:
            @pl.when((i == ni - 1) & (j == nj - 1) & (kk == nk - 1))
            def _():
                _gather_finish(x_refs, g_refs, *sems)

    res = pl.pallas_call(
        body, grid=(ni, nj, nk), in_specs=[a_spec, b_spec] + _any_specs(n_x),
        out_specs=[pl.BlockSpec((tm, tn), lambda i, j, kk: (i, j))] + _any_specs(n_x),
        out_shape=[SDS((m, n), out_dtype)] + [SDS((N_DEV,) + x.shape, x.dtype) for x in carry],
        scratch_shapes=[pltpu.VMEM((tm, tn), F32)] + (_gather_sems(n_x) if n_x else []),
        compiler_params=_cp("arbitrary", "arbitrary", "arbitrary") if n_x else _cp("parallel", "parallel", "arbitrary"),
        name=name)(a, b, *carry)
    return (res[0], list(res[1:])) if n_x else res[0]


def pmatmul(a, w, name, carry=None):
    if carry:
        @jax.custom_vjp
        def cop(a, w, *xs):
            y, got = _mm(a, w, "nn", name + "_fwd", carry=xs)
            return y, tuple(got)

        def cfwd(a, w, *xs):
            return cop(a, w, *xs), (a, w, xs)

        def cbwd(res, cts):
            a, w, xs = res
            return (_mm(cts[0], w, "nt", name + "_dx"), _mm(a, cts[0], "tn", name + "_dw", out_dtype=w.dtype),
                    *[jnp.zeros_like(x) for x in xs])

        cop.defvjp(cfwd, cbwd)
        y, got = cop(a, w, *carry)
        return y, list(got)

    @jax.custom_vjp
    def op(a, w):
        return _mm(a, w, "nn", name + "_fwd")

    def fwd(a, w):
        return _mm(a, w, "nn", name + "_fwd"), (a, w)

    def bwd(res, g):
        a, w = res
        return _mm(g, w, "nt", name + "_dx"), _mm(a, g, "tn", name + "_dw", out_dtype=w.dtype)

    op.defvjp(fwd, bwd)
    return op(a, w)


def rowop(f, name, rows, pars, out_ws, tile, ctx_rows=0):
    t = rows[0].shape[0]
    tile = math.gcd(math.gcd(tile, t), ctx_rows if ctx_rows else t)
    row_bytes = 2 * 4 * (2 * sum(r.shape[1] for r in rows) + sum(out_ws))
    while tile > 8 and tile % 2 == 0 and tile * row_bytes > ROW_PIPELINE_BYTES:
        tile //= 2
    nt = t // tile
    cut = ctx_rows // tile
    n_r, n_p, n_o = len(rows), len(pars), len(out_ws)
    segs = [p.shape[0] for p in pars]
    assert all(s in (1, 2) for s in segs)

    def par_map(s):
        if s == 1:
            return lambda i: (0, 0, 0)
        return lambda i: (jnp.where(i >= cut, 1, 0), 0, 0)

    row_specs = [pl.BlockSpec((tile, r.shape[1]), lambda i: (i, 0)) for r in rows]
    par_specs = [pl.BlockSpec((None, 1, p.shape[2]), par_map(p.shape[0])) for p in pars]
    out_specs = [pl.BlockSpec((tile, w), lambda i: (i, 0)) for w in out_ws]
    out_shape = [SDS((t, w), F32) for w in out_ws]

    def fwd_call(*args):
        def body(*refs):
            outs = f(*[r[...] for r in refs[:n_r + n_p]])
            for o_ref, o in zip(refs[n_r + n_p:], outs):
                o_ref[...] = o

        return tuple(pl.pallas_call(body, grid=(nt,), in_specs=row_specs + par_specs, out_specs=out_specs,
                                    out_shape=out_shape, compiler_params=_cp("parallel"), name=name + "_fwd")(*args))

    def bwd_call(args, gouts):
        def body(*refs):
            ins = [r[...] for r in refs[:n_r + n_p]]
            gs = tuple(r[...] for r in refs[n_r + n_p:n_r + n_p + n_o])
            d_refs = refs[n_r + n_p + n_o:]
            _, vjp = jax.vjp(f, *ins)
            grads = vjp(gs)
            for ref, g in zip(d_refs[:n_r], grads[:n_r]):
                ref[...] = g
            i = pl.program_id(0)
            for ref, g, s in zip(d_refs[n_r:], grads[n_r:], segs):
                first = (i == 0) if s == 1 else jnp.logical_or(i == 0, i == cut)

                @pl.when(first)
                def _(ref=ref, g=g):
                    ref[...] = g

                @pl.when(jnp.logical_not(first))
                def _(ref=ref, g=g):
                    ref[...] += g

        d_specs = row_specs + par_specs
        d_shape = [SDS(r.shape, F32) for r in rows] + [SDS(p.shape, F32) for p in pars]
        return tuple(pl.pallas_call(body, grid=(nt,), in_specs=row_specs + par_specs + out_specs, out_specs=d_specs,
                                    out_shape=d_shape, compiler_params=_cp("arbitrary"), name=name + "_bwd")(*args, *gouts))

    @jax.custom_vjp
    def op(*args):
        return fwd_call(*args)

    op.defvjp(lambda *args: (fwd_call(*args), args), bwd_call)
    return op(*rows, *pars)


def scanop(chunk_fn, name, cols, col_ws, grps, state_shape, y_w, rev=False, ctx_rows=0, unroll=1, lockstep=False,
           carry=None):
    carry = list(carry or [])
    n_x = len(carry)
    t = cols[0].shape[0]
    g_n = cols[0].shape[1] // col_ws[0]
    u_n = unroll if g_n % unroll == 0 else 1
    gs_n = g_n // u_n
    nc = t // CHUNK
    ncx = ctx_rows // CHUNK
    n_c, n_g = len(cols), len(grps)

    def order(c):
        if not rev:
            return c
        return jnp.where(c < ncx, ncx - 1 - c, nc - 1 - (c - ncx))

    def specs(cmap):
        col_specs = [pl.BlockSpec((CHUNK, u_n * w), lambda g, c: (cmap(c), g)) for w in col_ws]
        grp_specs = [pl.BlockSpec((u_n, CHUNK, a.shape[2]), lambda g, c: (g, cmap(c), 0)) for a in grps]
        y_spec = pl.BlockSpec((CHUNK, u_n * y_w), lambda g, c: (cmap(c), g))
        s_spec = pl.BlockSpec((u_n, None) + state_shape, lambda g, c: (g, cmap(c), 0, 0))
        return col_specs, grp_specs, y_spec, s_spec

    def group_ins(ins, u):
        return ([a[:, u * w:(u + 1) * w] for a, w in zip(ins[:n_c], col_ws)] + [a[u] for a in ins[n_c:]])

    if lockstep:
        step_fn = chunk_fn
    else:
        def step_fn(s_list, ins_list):
            return [chunk_fn(s, *ins) for s, ins in zip(s_list, ins_list)]

    def fwd_call(*args):
        col_specs, grp_specs, y_spec, s_spec = specs(order)

        def body(*refs):
            n_in = n_c + n_g
            ins = [r[...] for r in refs[:n_in]]
            x_refs = refs[n_in:n_in + n_x]
            y_ref, sall_ref = refs[n_in + n_x:n_in + n_x + 2]
            g_refs = refs[n_in + n_x + 2:n_in + 2 * n_x + 2]
            s_ref = refs[n_in + 2 * n_x + 2]
            sems = refs[n_in + 2 * n_x + 3:]

            if n_x:
                @pl.when((pl.program_id(0) == 0) & (pl.program_id(1) == 0))
                def _():
                    _gather_start(x_refs, g_refs, *sems)

            @pl.when(pl.program_id(1) == 0)
            def _():
                s_ref[...] = jnp.zeros_like(s_ref)

            s = s_ref[...]
            sall_ref[...] = s
            res = step_fn([s[u] for u in range(u_n)], [group_ins(ins, u) for u in range(u_n)])
            y_ref[...] = jnp.concatenate([y for _, y in res], axis=1)
            s_ref[...] = jnp.stack([s_new for s_new, _ in res])

            if n_x:
                @pl.when((pl.program_id(0) == gs_n - 1) & (pl.program_id(1) == nc - 1))
                def _():
                    _gather_finish(x_refs, g_refs, *sems)

        return pl.pallas_call(
            body, grid=(gs_n, nc), in_specs=col_specs + grp_specs + _any_specs(n_x),
            out_specs=[y_spec, s_spec] + _any_specs(n_x),
            out_shape=[SDS((t, g_n * y_w), F32), SDS((g_n, nc) + state_shape, F32)]
            + [SDS((N_DEV,) + x.shape, x.dtype) for x in carry],
            scratch_shapes=[pltpu.VMEM((u_n,) + state_shape, F32)] + (_gather_sems(n_x) if n_x else []),
            compiler_params=_cp("arbitrary" if n_x else "parallel", "arbitrary"), name=name + "_fwd")(*args)

    def bwd_call(res, ct):
        args, s_all = res
        gy = ct[0] if n_x else ct
        args, xs = args[:n_c + n_g], args[n_c + n_g:]
        col_specs, grp_specs, y_spec, s_spec = specs(lambda c: order(nc - 1 - c))

        def body(*refs):
            ins = [r[...] for r in refs[:n_c + n_g]]
            s_prev = refs[n_c + n_g][...]
            dy = refs[n_c + n_g + 1][...]
            d_refs = refs[n_c + n_g + 2:-1]
            ds_ref = refs[-1]

            @pl.when(pl.program_id(1) == 0)
            def _():
                ds_ref[...] = jnp.zeros_like(ds_ref)

            ds = ds_ref[...]
            _, vjp = jax.vjp(step_fn, [s_prev[u] for u in range(u_n)], [group_ins(ins, u) for u in range(u_n)])
            d_s, d_ins = vjp([(ds[u], dy[:, u * y_w:(u + 1) * y_w]) for u in range(u_n)])
            grads = [[d_s[u]] + list(d_ins[u]) for u in range(u_n)]
            ds_ref[...] = jnp.stack([g[0] for g in grads])
            for j, ref in enumerate(d_refs):
                ref[...] = (jnp.concatenate([g[1 + j] for g in grads], axis=1) if j < n_c
                            else jnp.stack([g[1 + j] for g in grads]))

        grads = pl.pallas_call(
            body, grid=(gs_n, nc), in_specs=col_specs + grp_specs + [s_spec, y_spec], out_specs=col_specs + grp_specs,
            out_shape=[SDS(a.shape, F32) for a in args],
            scratch_shapes=[pltpu.VMEM((u_n,) + state_shape, F32)],
            compiler_params=_cp("parallel", "arbitrary"), name=name + "_bwd")(*args, s_all, gy)
        return tuple(grads) + tuple(jnp.zeros_like(x) for x in xs)

    def outputs(res):
        return (res[0], tuple(res[2:])) if n_x else res[0]

    @jax.custom_vjp
    def op(*args):
        return outputs(fwd_call(*args))

    def fwd(*args):
        res = fwd_call(*args)
        return outputs(res), (args, res[1])

    op.defvjp(fwd, bwd_call)
    out = op(*cols, *grps, *carry)
    return (out[0], list(out[1])) if n_x else out


def _dg(a, b, ca, cb, prec=None):
    return lax.dot_general(a, b, (((ca,), (cb,)), ((), ())), precision=prec, preferred_element_type=F32)


@functools.partial(jax.custom_vjp, nondiff_argnums=(2, 3))
def mmb(a, b, ca, cb):
    return _dg(a.astype(BF16), b.astype(BF16), ca, cb)


def _mmb_fwd(a, b, ca, cb):
    return mmb(a, b, ca, cb), (a, b)


def _mmb_bwd(ca, cb, res, g):
    a, b = res
    ab, bb, gb = a.astype(BF16), b.astype(BF16), g.astype(BF16)
    da = _dg(gb, bb, 1, 1 - cb) if ca == 1 else _dg(bb, gb, 1 - cb, 1)
    db = _dg(ab, gb, 1 - ca, 0) if cb == 0 else _dg(gb, ab, 0, 1 - ca)
    return da, db


mmb.defvjp(_mmb_fwd, _mmb_bwd)


def mmh(a, b, ca=1, cb=0):
    return _dg(a, b, ca, cb, HI)


def _masks(n, rev=False):
    i = lax.broadcasted_iota(jnp.int32, (n, n), 0)
    j = lax.broadcasted_iota(jnp.int32, (n, n), 1)
    return (i <= j, i < j) if rev else (i >= j, i > j)


def _silu(x):
    return x * jax.nn.sigmoid(x)


def _softplus(x):
    return jnp.maximum(x, 0.0) + jnp.log(1.0 + jnp.exp(-jnp.abs(x)))


def _expand_heads(p, width):
    h = p.shape[1]
    lane = lax.broadcasted_iota(jnp.int32, (h, h * width), 1)
    row = lax.broadcasted_iota(jnp.int32, (h, h * width), 0)
    e = jnp.where(lane // width == row, 1.0, 0.0).astype(F32)
    return mmh(jnp.broadcast_to(p, (8, h)), e)[0:1]


def _group_apply(x, width, fn):
    n = x.shape[1] // width
    return jnp.concatenate([fn(x[:, g * width:(g + 1) * width]) for g in range(n)], axis=1)


def _rms(x):
    return x * lax.rsqrt(jnp.mean(x * x, axis=-1, keepdims=True) + NORM_EPS)


@jax.custom_vjp
def unit_lower_inv(ls):
    n = ls[0].shape[0]
    eye = jnp.where(_masks(n)[0] & jnp.logical_not(_masks(n)[1]), 1.0, 0.0).astype(F32)
    ps = [-l for l in ls]
    xs = [eye + p for p in ps]
    for _ in range(int(math.log2(n)) - 1):
        ps = [mmh(p, p) for p in ps]
        xs = [x + mmh(x, p) for x, p in zip(xs, ps)]
    return tuple(xs)


def _uli_fwd(ls):
    xs = unit_lower_inv(ls)
    return xs, xs


def _uli_bwd(xs, gs):
    ts = [_dg(x, g, 0, 0, HI) for x, g in zip(xs, gs)]
    return (tuple(-_dg(t, x, 1, 1, HI) for t, x in zip(ts, xs)),)


unit_lower_inv.defvjp(_uli_fwd, _uli_bwd)


def scalar_decay_chunk(r_n, n, p, with_dt, rev):
    def fn(s, q, k, v, la, *rest):
        c = q.shape[0]
        incl, _ = _masks(c, rev)
        tril = incl.astype(F32)
        cum = mmh(tril, la)
        cum_t = _dg(la, tril, 0, 1, HI)
        tot = cum[0:1, :] if rev else cum[c - 1:c, :]
        scores = mmb(q, k, 1, 1)
        ys, ss = [], []
        for r in range(r_n):
            cr = cum[:, r:r + 1]
            dec = jnp.exp(jnp.where(incl, cr - cum_t[r:r + 1, :], -jnp.inf))
            vr = v[:, r * p:(r + 1) * p]
            if with_dt:
                vr = vr * rest[0][:, r:r + 1]
            sr = s[r * n:(r + 1) * n, :]
            ys.append(mmb(scores * dec, vr, 1, 0) + mmb(q, sr, 1, 0) * jnp.exp(cr))
            ss.append(jnp.exp(tot[:, r:r + 1]) * sr + mmb(k, jnp.exp(tot[:, r:r + 1] - cr) * vr, 0, 0))
        return jnp.concatenate(ss, axis=0), jnp.concatenate(ys, axis=1)

    return fn


def vector_decay_chunk(rev):
    return functools.partial(_vector_decay_chunk, rev)


def _vector_decay_chunk(rev, s, q, k, v, lf):
    c, kd = q.shape
    incl, _ = _masks(c, rev)
    cum = mmh(incl.astype(F32), lf)
    tot = cum[0:1, :] if rev else cum[c - 1:c, :]
    tot_col = _dg(lf, jnp.ones((c, 1), F32), 0, 0, HI)
    y_inter = mmb(q * jnp.exp(cum), s, 1, 0)
    s_new = jnp.exp(tot_col) * s + mmb(k * jnp.exp(tot - cum), v, 0, 0)
    i3 = lax.broadcasted_iota(jnp.int32, (SUB, SUB, 1), 0)
    j3 = lax.broadcasted_iota(jnp.int32, (SUB, SUB, 1), 1)
    ys = []
    for b in range(c // SUB):
        lo, hi = b * SUB, (b + 1) * SUB
        qb, kb, cb, vb = q[lo:hi], k[lo:hi], cum[lo:hi], v[lo:hi]
        dec = jnp.exp(jnp.where((i3 <= j3) if rev else (i3 >= j3), cb[:, None, :] - cb[None, :, :], -jnp.inf))
        a_diag = jnp.sum(qb[:, None, :] * kb[None, :, :] * dec, axis=-1)
        yb = mmb(a_diag, vb, 1, 0)
        if rev and hi < c:
            base = cum[hi:hi + 1, :]
            a_off = mmb(qb * jnp.exp(cb - base), k[hi:] * jnp.exp(base - cum[hi:]), 1, 1)
            yb = yb + mmb(a_off, v[hi:], 1, 0)
        if not rev and b > 0:
            base = cum[lo - 1:lo, :]
            a_off = mmb(qb * jnp.exp(cb - base), k[:lo] * jnp.exp(base - cum[:lo]), 1, 1)
            yb = yb + mmb(a_off, v[:lo], 1, 0)
        ys.append(yb)
    return s_new, y_inter + jnp.concatenate(ys, axis=0)


def delta_chunk(r_n, kd, vd, rev):
    def fn(s_list, ins_list):
        c = ins_list[0][0].shape[0]
        incl, strict = _masks(c, rev)
        tril = incl.astype(F32)
        heads = [(g, r) for g in range(len(ins_list)) for r in range(r_n)]
        q = [ins[0] for ins in ins_list]
        k = [ins[1] for ins in ins_list]
        cum = [mmh(tril, ins[4]) for ins in ins_list]
        cum_t = [_dg(ins[4], tril, 0, 1, HI) for ins in ins_list]
        kk = [mmb(a, a, 1, 1) for a in k]
        qk = [mmb(a, b, 1, 1) for a, b in zip(q, k)]
        cr = [cum[g][:, r:r + 1] for g, r in heads]
        br = [ins_list[g][3][:, r:r + 1] for g, r in heads]
        seg = [c_ - cum_t[g][r:r + 1, :] for c_, (g, r) in zip(cr, heads)]
        tot = [(cum[g][0:1, :] if rev else cum[g][c - 1:c, :])[:, r:r + 1] for g, r in heads]
        vr = [ins_list[g][2][:, r * vd:(r + 1) * vd] for g, r in heads]
        sr = [s_list[g][r * kd:(r + 1) * kd, :] for g, r in heads]
        xs = unit_lower_inv(tuple(b * kk[g] * jnp.exp(jnp.where(strict, sg, -jnp.inf))
                                  for b, sg, (g, r) in zip(br, seg, heads)))
        u = [mmh(x, v_ * b) for x, v_, b in zip(xs, vr, br)]
        w = [mmh(x, k[g] * (b * jnp.exp(c_))) for x, b, c_, (g, r) in zip(xs, br, cr, heads)]
        ws = [mmb(w_, s_, 1, 0) for w_, s_ in zip(w, sr)]
        v_new = [u_ - a for u_, a in zip(u, ws)]
        ya = [mmb(qk[g] * jnp.exp(jnp.where(incl, sg, -jnp.inf)), vn, 1, 0) for sg, vn, (g, r) in zip(seg, v_new, heads)]
        yb = [mmb(q[g], s_, 1, 0) * jnp.exp(c_) for s_, c_, (g, r) in zip(sr, cr, heads)]
        sn = [jnp.exp(t_) * s_ + mmb(k[g], jnp.exp(t_ - c_) * vn, 0, 0)
              for t_, s_, c_, vn, (g, r) in zip(tot, sr, cr, v_new, heads)]
        out = []
        for g in range(len(ins_list)):
            idx = [i for i, (g2, _) in enumerate(heads) if g2 == g]
            out.append((jnp.concatenate([sn[i] for i in idx], axis=0), jnp.concatenate([ya[i] + yb[i] for i in idx], axis=1)))
        return out

    return fn


def f_adaln(x, g, sh, sc):
    return ((_rms(x) * g) * (1.0 + sc) + sh,)


def f_resid(x, y, gate):
    return (x + gate * y,)


def f_relu2(h):
    return (jnp.square(jnp.maximum(h, 0.0)),)


def f_conv_bias(up, u, un, w0, w1, w2, b):
    return (_silu(w0 * up + w1 * u + w2 * un + b),)


def f_conv(up, u, un, w0, w1, w2):
    return (_silu(w0 * up + w1 * u + w2 * un),)


def f_dt(raw, bias, a_log):
    dt = _softplus(raw + bias)
    return dt, -jnp.exp(a_log) * dt


def f_ssd_post(yf, yb, xs, z, d_skip, ng):
    y = (yf + yb + _expand_heads(d_skip, SSD_HEAD_DIM) * xs) * _silu(z)
    return (_group_apply(y, y.shape[1] // SSD_GROUPS, _rms) * ng,)


def f_rope(scale):
    def f(q, qsw, cos, sin):
        return ((q * cos + qsw * sin) * scale,)
    return f


def f_ret_post(yf, yb, g):
    def ln(y):
        mu = jnp.mean(y, axis=-1, keepdims=True)
        var = jnp.mean(jnp.square(y - mu), axis=-1, keepdims=True)
        return (y - mu) * lax.rsqrt(var + NORM_EPS)
    y = yf + yb
    return (_group_apply(y, y.shape[1] // RET_HEADS, ln) * _silu(g),)


def f_lower_bound(layer):
    def f(logits):
        m = jnp.max(logits, axis=0, keepdims=True)
        e = jnp.exp(logits - m)
        p = e / jnp.sum(e, axis=0, keepdims=True)
        lb = p[1:2]
        for j in range(2, layer + 1):
            lb = lb + p[j:j + 1]
        return (jnp.broadcast_to(lb, logits.shape),)
    return f


def f_hgrn_gates(ff, fb, lb):
    def gates(f):
        log_sig = jnp.minimum(f, 0.0) - jnp.log(1.0 + jnp.exp(-jnp.abs(f)))
        a, b = jnp.log(lb), jnp.log(1.0 - lb) + log_sig
        m = jnp.maximum(a, b)
        return m + jnp.log(jnp.exp(a - m) + jnp.exp(b - m)), (1.0 - lb) * jax.nn.sigmoid(-f)
    lf_f, k_f = gates(ff)
    lf_b, k_b = gates(fb)
    return lf_f, k_f, lf_b, k_b


def f_hgrn_post(yf, yb, g, ng):
    return (_group_apply(yf + yb, HEAD128, _rms) * ng * _silu(g),)


def f_gdn_qk(q, k):
    def l2(x):
        return x * lax.rsqrt(jnp.sum(x * x, axis=-1, keepdims=True) + 1e-6)
    return _group_apply(q, HEAD128, l2) * HEAD128 ** -0.5, _group_apply(k, HEAD128, l2)


def f_gdn_gates(bt, a, dt_bias, a_log):
    return jax.nn.sigmoid(bt), -jnp.exp(a_log) * _softplus(a + dt_bias)


def f_gdn_post(yf, yb, z, ng):
    n = yf.shape[1] // HEAD128
    return (_group_apply(yf + yb, HEAD128, _rms) * jnp.concatenate([ng] * n, axis=1) * _silu(z),)


def f_loss(x, tgt, fg):
    err = jnp.square(_rms(x) * fg - tgt)
    return (jnp.broadcast_to(jnp.mean(err, axis=-1, keepdims=True), (x.shape[0], LANES)),)


def _par(v):
    return v.reshape(1, 1, -1)


def _shift(u, lc):
    z = jnp.zeros((1, u.shape[1]), u.dtype)
    prev = jnp.concatenate([z, u[:lc - 1], z, u[lc:-1]], axis=0)
    nxt = jnp.concatenate([u[1:lc], z, u[lc + 1:], z], axis=0)
    return prev, nxt


def _grp(a, g_n):
    t = a.shape[0]
    return a.reshape(t, g_n, -1).transpose(1, 0, 2)


def ssd_mixer(h, p, lc, start, carry):
    d = h.shape[1]
    di = 2 * d
    gn = SSD_GROUPS * SSD_STATE
    heads = di // SSD_HEAD_DIM
    r_n = heads // SSD_GROUPS
    w_z, w_x, w_dt = split_cols(p["ssd_w_in"], [di, 2 * di + 2 * gn])
    xbc, got0 = cmm(h, w_x, "ssd_in_x", carry[0])
    z, got1 = cmm(h, w_z, "ssd_in_z", carry[1])
    dtr = pmatmul(h, w_dt, "ssd_in_dt")
    up, un = _shift(xbc, lc)
    cw = p["ssd_conv_w"]
    (xbc,) = rowop(f_conv_bias, "ssd_conv", [up, xbc, un], [_par(cw[0]), _par(cw[1]), _par(cw[2]), _par(p["ssd_conv_b"])],
                   [xbc.shape[1]], 128)
    xs, bm, cm = xbc[:, :di], xbc[:, di:di + gn], xbc[:, di + gn:]
    dt, la = rowop(f_dt, "ssd_dt", [dtr], [_par(p["ssd_dt_bias"]), _par(p["ssd_a_log"])], [2 * heads, 2 * heads], 256)

    def scan(tag, rev, grps, cr):
        return _with_got(scanop(scalar_decay_chunk(r_n, SSD_STATE, SSD_HEAD_DIM, True, rev), "ssd_scan_" + tag,
                                [cm, bm, xs], [SSD_STATE, SSD_STATE, r_n * SSD_HEAD_DIM], grps,
                                (r_n * SSD_STATE, SSD_HEAD_DIM), r_n * SSD_HEAD_DIM, rev, lc, carry=cr), cr)

    yf, got2 = scan("f", False, [_grp(la[:, :heads], SSD_GROUPS), _grp(dt[:, :heads], SSD_GROUPS)], carry[2])
    yb, got3 = scan("b", True, [_grp(la[:, heads:], SSD_GROUPS), _grp(dt[:, heads:], SSD_GROUPS)], carry[3])
    (yn,) = rowop(f_ssd_post, "ssd_post", [yf[start:], yb[start:], xs[start:], z[start:]],
                  [_par(p["ssd_d"]), _par(p["ssd_norm_g"])], [di], 128)
    return pmatmul(yn, p["ssd_w_out"], "ssd_out"), got0 + got1 + got2 + got3


def _rope_tables(n_lat, lc, d, heads):
    qk = d // heads
    half = qk // 2
    quarter = half // 2
    pos = jnp.arange(n_lat)
    inv_freq = ROPE_BASE ** (-jnp.arange(0, half, 2, dtype=F32) / half)

    def tab(p):
        ang = p.astype(F32)[:, None] * inv_freq
        return jnp.cos(ang), jnp.sin(ang)

    cr, sr = tab(pos // GRID_W)
    cc, sc = tab(pos % GRID_W)
    cos = jnp.concatenate([cr, cr, cc, cc], axis=1)
    sin = jnp.concatenate([-sr, sr, -sc, sc], axis=1)
    cos = jnp.concatenate([jnp.ones((lc, qk), F32), cos], axis=0)
    sin = jnp.concatenate([jnp.zeros((lc, qk), F32), sin], axis=0)
    return jnp.tile(cos, (1, heads)), jnp.tile(sin, (1, heads)), quarter


def retention_mixer(h, p, lc, start, carry):
    t, d = h.shape
    dv = 2 * d
    qk = d // RET_HEADS
    w_q, w_k, w_v, w_g = split_cols(p["ret_w_in"], [d, 2 * d, 2 * d + dv])
    v, got0 = cmm(h, w_v, "ret_in_v", carry[0])
    g, got1 = cmm(h, w_g, "ret_in_g", carry[1])
    q, k = pmatmul(h, w_q, "ret_in_q"), pmatmul(h, w_k, "ret_in_k")
    cos, sin, quarter = _rope_tables(t - lc, lc, d, RET_HEADS)
    cos, sin = lax.stop_gradient(cos), lax.stop_gradient(sin)

    def swap(a):
        return a.reshape(t, -1, 2, quarter)[:, :, ::-1, :].reshape(t, d)

    (q,) = rowop(f_rope(1.0), "ret_rope_q", [q, swap(q), cos, sin], [], [d], 256)
    (k,) = rowop(f_rope(qk ** -0.5), "ret_rope_k", [k, swap(k), cos, sin], [], [d], 256)

    def scan(tag, rev, grps, cr):
        return _with_got(scanop(scalar_decay_chunk(1, qk, dv // RET_HEADS, False, rev), "ret_scan_" + tag, [q, k, v],
                                [qk, qk, dv // RET_HEADS], grps, (qk, dv // RET_HEADS), dv // RET_HEADS, rev, lc,
                                GROUPS_PER_STEP, carry=cr), cr)

    ld = p["ret_log_decay"]
    la_f = jnp.broadcast_to(ld[0][:, None, None], (RET_HEADS, t, 1))
    la_b = jnp.broadcast_to(ld[1][:, None, None], (RET_HEADS, t, 1))
    (yf, got2), (yb, got3) = scan("f", False, [la_f], carry[2]), scan("b", True, [la_b], carry[3])
    (yn,) = rowop(f_ret_post, "ret_post", [yf[start:], yb[start:], g[start:]], [], [dv], 128)
    return pmatmul(yn, p["ret_w_out"], "ret_out"), got0 + got1 + got2 + got3


def hgrn2_mixer(h, p, lc, start, layer, carry):
    t, d = h.shape
    w_q, w_ff, w_fb, w_i, w_g = split_cols(p["hgrn_w_in"], [d, 2 * d, 3 * d, 4 * d])
    q, got0 = cmm(h, w_q, "hgrn_in_q", carry[0])
    f_f, got1 = cmm(h, w_ff, "hgrn_in_ff", carry[1])
    f_b = pmatmul(h, w_fb, "hgrn_in_fb")
    i, g = pmatmul(h, w_i, "hgrn_in_i"), pmatmul(h, w_g, "hgrn_in_g")
    (lb,) = rowop(f_lower_bound(layer), "hgrn_lb", [p["hgrn_lb_logits"]], [], [d], DEPTH)
    lf_f, k_f, lf_b, k_b = rowop(f_hgrn_gates, "hgrn_gates", [f_f, f_b], [lb[0:1][None]], [d] * 4, 256)

    def scan(tag, rev, cols, cr):
        return _with_got(scanop(vector_decay_chunk(rev), "hgrn_scan_" + tag, cols, [HEAD128] * 4, [], (HEAD128, HEAD128),
                                HEAD128, rev, lc, GROUPS_PER_STEP, carry=cr), cr)

    yf, got2 = scan("f", False, [q, k_f, i, lf_f], carry[2])
    yb, got3 = scan("b", True, [q, k_b, i, lf_b], carry[3])
    (yn,) = rowop(f_hgrn_post, "hgrn_post", [yf[start:], yb[start:], g[start:]], [_par(p["hgrn_norm_g"])], [d], 256)
    return pmatmul(yn, p["hgrn_w_out"], "hgrn_out"), got0 + got1 + got2 + got3


def gdn_mixer(h, p, lc, start):
    t, d = h.shape
    dk, dv = d, 2 * d
    kh = d // HEAD128
    r_n = 2
    cc = 2 * dk + dv
    nb = 2 * kh * r_n
    w_qkv, w_z, w_bt, w_a = split_cols(p["gdn_w_in"], [cc, cc + dv, cc + dv + nb])
    qkv, z = pmatmul(h, w_qkv, "gdn_in_qkv"), pmatmul(h, w_z, "gdn_in_z")
    bt, a = pmatmul(h, w_bt, "gdn_in_bt"), pmatmul(h, w_a, "gdn_in_a")
    up, un = _shift(qkv, lc)
    cw = p["gdn_conv_w"]
    (qkv,) = rowop(f_conv, "gdn_conv", [up, qkv, un], [_par(cw[0]), _par(cw[1]), _par(cw[2])], [cc], 128)
    q, k, v = qkv[:, :dk], qkv[:, dk:2 * dk], qkv[:, 2 * dk:]
    q, k = rowop(f_gdn_qk, "gdn_qk", [q, k], [], [dk, dk], 256)
    beta, la = rowop(f_gdn_gates, "gdn_gates", [bt, a], [_par(p["gdn_dt_bias"]), _par(p["gdn_a_log"])], [nb, nb], 256)

    def scan(tag, rev, grps):
        return scanop(delta_chunk(r_n, HEAD128, HEAD128, rev), "gdn_scan_" + tag, [q, k, v],
                      [HEAD128, HEAD128, r_n * HEAD128], grps, (r_n * HEAD128, HEAD128), r_n * HEAD128, rev, lc,
                      GDN_GROUPS_PER_STEP, True)

    half = nb // 2
    yf = scan("f", False, [_grp(beta[:, :half], kh), _grp(la[:, :half], kh)])
    yb = scan("b", True, [_grp(beta[:, half:], kh), _grp(la[:, half:], kh)])
    (yn,) = rowop(f_gdn_post, "gdn_post", [yf[start:], yb[start:], z[start:]], [_par(p["gdn_norm_g"])], [dv], 128)
    return pmatmul(yn, p["gdn_w_out"], "gdn_out")


@jax.custom_vjp
def tap(w, probe):
    return w


tap.defvjp(lambda w, probe: (w, None), lambda _, g: (jnp.zeros_like(g), g))


def split_cols(w, cuts):
    bounds = list(zip([0] + list(cuts), list(cuts) + [w.shape[1]]))

    @jax.custom_vjp
    def f(w):
        return tuple(w[:, a:b] for a, b in bounds)

    f.defvjp(lambda w: (f(w), None), lambda _, gs: (jnp.concatenate(gs, axis=1),))
    return f(w)


def _with_got(res, carry):
    return res if carry else (res, [])


def cmm(a, w, name, carry):
    if carry:
        return pmatmul(a, w, name, carry)
    return pmatmul(a, w, name), []


MIXER_W = [("ssd_w_in", "ssd_w_out"), ("ret_w_in", "ret_w_out"), ("hgrn_w_in", "hgrn_w_out"), ("gdn_w_in", "gdn_w_out")]


def local_loss(x, ctx, mod2, tgt, p, shards, shapes):
    lc, d = ctx.shape
    p = dict(p)
    xc = jnp.concatenate([ctx, x], axis=0)
    for i in range(DEPTH):
        keep = i < DEPTH - 1
        start = 0 if keep else lc
        carry, out_carry = [None] * 4, None
        if keep:
            w_in_n, w_out_n = MIXER_W[i + 1]
            half = shards[w_in_n].shape[0] // 2
            carry = [[shards[w_in_n][:half]], [shards[w_in_n][half:]], [shards["mlp_w1_%d" % (i + 1)]],
                     [shards["mlp_w2_%d" % (i + 1)]]]
            out_carry = [shards[w_out_n]]
        sh1, sc1, g1, sh2, sc2, g2 = (mod2[i][:, j * d:(j + 1) * d][:, None, :] for j in range(6))
        (h,) = rowop(f_adaln, "adaln_a%d" % i, [xc], [_par(p["norm_g"][i, 0]), sh1, sc1], [d], 256, lc)
        if i % 4 == 0:
            y, got = ssd_mixer(h, p, lc, start, carry)
        elif i % 4 == 1:
            y, got = retention_mixer(h, p, lc, start, carry)
        elif i % 4 == 2:
            y, got = hgrn2_mixer(h, p, lc, start, i, carry)
        else:
            y, got = gdn_mixer(h, p, lc, start), []
        if not keep:
            xc, g1, sh2, sc2, g2, lc = xc[lc:], g1[1:], sh2[1:], sc2[1:], g2[1:], 0
        (xc,) = rowop(f_resid, "resid_a%d" % i, [xc, y], [g1], [d], 256, lc)
        (h2,) = rowop(f_adaln, "adaln_b%d" % i, [xc], [_par(p["norm_g"][i, 1]), sh2, sc2], [d], 256, lc)
        a, got_out = cmm(h2, p["mlp_w1_%d" % i], "mlp1_%d" % i, out_carry)
        (a,) = rowop(f_relu2, "relu2_%d" % i, [a], [], [a.shape[1]], 128)
        m = pmatmul(a, p["mlp_w2_%d" % i], "mlp2_%d" % i)
        (xc,) = rowop(f_resid, "resid_b%d" % i, [xc, m], [g2], [d], 256, lc)
        if keep:
            arrived = [(w_in_n, jnp.concatenate(got[:2], axis=1)), ("mlp_w1_%d" % (i + 1), got[2]),
                       ("mlp_w2_%d" % (i + 1), got[3]), (w_out_n, got_out[0])]
            for n, g8 in arrived:
                shp, axis = shapes[n]
                p[n] = tap(_to_full(g8.reshape((N_DEV,) + shp), axis)[0], p[n])
    (rows,) = rowop(f_loss, "loss", [xc, tgt], [_par(p["final_g"])], [LANES], 256)
    return 0.5 * jnp.sum(rows[:, 0])


def exchange(x, name):
    def body(x_ref, o_ref, send_sems, recv_sems, local_sem):
        mx, my, mc = lax.axis_index("x"), lax.axis_index("y"), lax.axis_index("c")
        me = 4 * mx + 2 * my + mc
        copies = []
        for k in range(1, N_DEV):
            px = 1 - mx if (k >> 2) & 1 else mx
            py = 1 - my if (k >> 1) & 1 else my
            pc = 1 - mc if k & 1 else mc
            cp = pltpu.make_async_remote_copy(src_ref=x_ref, dst_ref=o_ref.at[me], send_sem=send_sems.at[k - 1],
                                              recv_sem=recv_sems.at[k - 1], device_id=(px, py, pc), device_id_type=MESH)
            cp.start()
            copies.append(cp)
        mine = pltpu.make_async_copy(x_ref, o_ref.at[me], local_sem)
        mine.start()
        for cp in copies:
            cp.wait_recv()
        for cp in copies:
            cp.wait_send()
        mine.wait()

    return pl.pallas_call(
        body, out_shape=SDS((N_DEV,) + x.shape, x.dtype),
        in_specs=[pl.BlockSpec(memory_space=pl.ANY)], out_specs=pl.BlockSpec(memory_space=pl.ANY),
        scratch_shapes=[pltpu.SemaphoreType.DMA((N_DEV - 1,)), pltpu.SemaphoreType.DMA((N_DEV - 1,)),
                        pltpu.SemaphoreType.DMA],
        name=name)(x)


def _place():
    mx, my, mc = lax.axis_index("x"), lax.axis_index("y"), lax.axis_index("c")
    chips = [(1 - mx, my), (mx, 1 - my), (1 - mx, 1 - my)]
    return mx, my, mc, chips


def _any_specs(n):
    return [pl.BlockSpec(memory_space=pl.ANY)] * n


def _gather_sems(n):
    return [pltpu.SemaphoreType.DMA((n, 7)), pltpu.SemaphoreType.DMA((n, 7)), pltpu.SemaphoreType.DMA((n,))]


def _gather_copies(x_refs, o_refs, send_sems, recv_sems, local_sems):
    mx, my, mc, chips = _place()
    me, sib = (mx, my, mc), (mx, my, 1 - mc)

    def copy(a, k, block, to, src=None):
        dst = o_refs[a].at[4 * block[0] + 2 * block[1] + block[2]]
        return pltpu.make_async_remote_copy(src_ref=dst if src is None else src, dst_ref=dst, send_sem=send_sems.at[a, k],
                                            recv_sem=recv_sems.at[a, k], device_id=to, device_id_type=MESH)

    n = len(x_refs)
    mine = [pltpu.make_async_copy(x_refs[a], o_refs[a].at[4 * mx + 2 * my + mc], local_sems.at[a]) for a in range(n)]
    own = [copy(a, 1 + j, me, (*chip, mc), src=x_refs[a]) for a in range(n) for j, chip in enumerate(chips)]
    own += [copy(a, 0, me, sib, src=x_refs[a]) for a in range(n)]
    return mc, chips, me, sib, copy, mine, own


def _gather_start(x_refs, o_refs, send_sems, recv_sems, local_sems):
    *_, mine, own = _gather_copies(x_refs, o_refs, send_sems, recv_sems, local_sems)
    for cp in mine + own:
        cp.start()


def _gather_finish(x_refs, o_refs, send_sems, recv_sems, local_sems):
    mc, chips, me, sib, copy, mine, own = _gather_copies(x_refs, o_refs, send_sems, recv_sems, local_sems)
    n = len(x_refs)
    passed = []
    for a in range(n):
        for j, chip in enumerate(chips):
            copy(a, 1 + j, (*chip, mc), me).wait_recv()
            fwd = copy(a, 4 + j, (*chip, mc), sib)
            fwd.start()
            passed.append(fwd)
    for a in range(n):
        copy(a, 0, sib, me).wait_recv()
        for j, chip in enumerate(chips):
            copy(a, 4 + j, (*chip, 1 - mc), me).wait_recv()
    for cp in own + passed:
        cp.wait_send()
    for cp in mine:
        cp.wait()


def gather_two_level(arrs, name):
    n = len(arrs)

    def body(*refs):
        _gather_start(refs[:n], refs[n:2 * n], *refs[2 * n:])
        _gather_finish(refs[:n], refs[n:2 * n], *refs[2 * n:])

    return pl.pallas_call(
        body, out_shape=[SDS((N_DEV,) + a.shape, a.dtype) for a in arrs], in_specs=_any_specs(n), out_specs=_any_specs(n),
        scratch_shapes=_gather_sems(n), name=name)(*arrs)


def scatter_sibling(parts, name):
    n = len(parts)

    def body(*refs):
        x_refs, o_refs = refs[:n], refs[n:2 * n]
        send_sems, recv_sems = refs[2 * n:]
        mx, my, mc, _ = _place()
        copies = []
        for a in range(n):
            for b in range(4):
                cp = pltpu.make_async_remote_copy(src_ref=x_refs[a].at[b, 1 - mc], dst_ref=o_refs[a].at[b],
                                                  send_sem=send_sems.at[a, b], recv_sem=recv_sems.at[a, b],
                                                  device_id=(mx, my, 1 - mc), device_id_type=MESH)
                cp.start()
                copies.append(cp)
        for cp in copies:
            cp.wait_recv()
        for cp in copies:
            cp.wait_send()

    return pl.pallas_call(
        body, out_shape=[SDS((4,) + p.shape[2:], p.dtype) for p in parts], in_specs=_any_specs(n), out_specs=_any_specs(n),
        scratch_shapes=[pltpu.SemaphoreType.DMA((n, 4)), pltpu.SemaphoreType.DMA((n, 4))], name=name)(*parts)


def pair_sum(mine, theirs, name):
    _, rows, cols = mine.shape
    tr = _pick(rows, max(8, (1 << 19) // cols), 8)

    def body(a_ref, b_ref, o_ref):
        o_ref[...] = (a_ref[...].astype(F32) + b_ref[...].astype(F32)).astype(o_ref.dtype)

    spec = pl.BlockSpec((None, tr, cols), lambda b, i: (b, i, 0))
    return pl.pallas_call(body, grid=(4, rows // tr), in_specs=[spec, spec], out_specs=spec,
                          out_shape=SDS(mine.shape, mine.dtype), compiler_params=_cp("parallel", "parallel"), name=name)(mine, theirs)


def scatter_chips(sums, name):
    n = len(sums)

    def body(*refs):
        x_refs, o_refs = refs[:n], refs[n:2 * n]
        send_sems, recv_sems, local_sems = refs[2 * n:]
        mx, my, mc, chips = _place()
        my_chip = 2 * mx + my
        mine = [pltpu.make_async_copy(x_refs[a].at[my_chip], o_refs[a].at[my_chip], local_sems.at[a]) for a in range(n)]
        for cp in mine:
            cp.start()
        copies = []
        for a in range(n):
            for j, (px, py) in enumerate(chips):
                cp = pltpu.make_async_remote_copy(src_ref=x_refs[a].at[2 * px + py], dst_ref=o_refs[a].at[my_chip],
                                                  send_sem=send_sems.at[a, j], recv_sem=recv_sems.at[a, j],
                                                  device_id=(px, py, mc), device_id_type=MESH)
                cp.start()
                copies.append(cp)
        for cp in copies:
            cp.wait_recv()
        for cp in copies:
            cp.wait_send()
        for cp in mine:
            cp.wait()

    return pl.pallas_call(
        body, out_shape=[SDS(p.shape, p.dtype) for p in sums], in_specs=_any_specs(n), out_specs=_any_specs(n),
        scratch_shapes=[pltpu.SemaphoreType.DMA((n, 3)), pltpu.SemaphoreType.DMA((n, 3)), pltpu.SemaphoreType.DMA((n,))],
        name=name)(*sums)


def sum_parts(parts, name):
    n_p, rows, _ = parts.shape
    tr = _pick(rows, 1024, 8)

    def body(p_ref, o_ref):
        acc = p_ref[0].astype(F32)
        for j in range(1, n_p):
            acc = acc + p_ref[j].astype(F32)
        o_ref[...] = acc

    return pl.pallas_call(body, grid=(rows // tr,), in_specs=[pl.BlockSpec((n_p, tr, LANES), lambda i: (0, i, 0))],
                          out_specs=pl.BlockSpec((tr, LANES), lambda i: (i, 0)), out_shape=SDS((rows, LANES), F32),
                          compiler_params=_cp("parallel"), name=name)(parts)


def adamw(w, parts, m, v, name):
    rows, cols = w.shape
    n_p = parts.shape[0]
    tr = _pick(rows, max(8, (1 << 18) // cols), 8)
    c1 = 1.0 - ADAM_B1 ** ADAM_STEP
    c2 = 1.0 - ADAM_B2 ** ADAM_STEP

    def body(w_ref, p_ref, m_ref, v_ref, g_out, d_out, m_out, v_out):
        g = p_ref[0].astype(F32)
        for j in range(1, n_p):
            g = g + p_ref[j].astype(F32)
        m_new = ADAM_B1 * m_ref[...] + (1.0 - ADAM_B1) * g
        v_new = ADAM_B2 * v_ref[...] + (1.0 - ADAM_B2) * jnp.square(g)
        g_out[...] = g
        m_out[...] = m_new
        v_out[...] = v_new
        d_out[...] = -ADAM_LR * ((m_new / c1) / (jnp.sqrt(v_new / c2) + ADAM_EPS) + ADAM_WD * w_ref[...])

    spec = pl.BlockSpec((tr, cols), lambda i: (i, 0))
    return pl.pallas_call(
        body, grid=(rows // tr,), in_specs=[spec, pl.BlockSpec((n_p, tr, cols), lambda i: (0, i, 0)), spec, spec],
        out_specs=[spec] * 4, out_shape=[SDS((rows, cols), F32)] * 4, compiler_params=_cp("parallel"), name=name)(w, parts, m, v)


FWD_NAMES = ["x", "c", "ctx", "c_ctx", "ada_w", "ada_b", "norm_g", "mlp_w1", "mlp_w2", "final_g", "ssd_w_in", "ssd_conv_w",
             "ssd_conv_b", "ssd_dt_bias", "ssd_a_log", "ssd_d", "ssd_norm_g", "ssd_w_out", "ret_w_in", "ret_log_decay",
             "ret_w_out", "hgrn_w_in", "hgrn_lb_logits", "hgrn_norm_g", "hgrn_w_out", "gdn_w_in", "gdn_conv_w", "gdn_dt_bias",
             "gdn_a_log", "gdn_norm_g", "gdn_w_out"]
WEIGHTS = FWD_NAMES[3:]
BIG = {"mlp_w1": 2, "mlp_w2": 1, "ssd_w_in": 2, "ssd_w_out": 1, "ret_w_in": 2, "ret_w_out": 1, "hgrn_w_in": 2,
       "hgrn_w_out": 1, "gdn_w_in": 2, "gdn_w_out": 1}
SMALL_SHARDED = ["norm_g", "ssd_conv_w", "gdn_conv_w", "hgrn_norm_g"]
SMALL = [n for n in WEIGHTS if n not in BIG and n != "ada_w"]


def _to_full(g8, axis):
    _, l, a, b = g8.shape
    if axis == 1:
        return g8.transpose(1, 0, 2, 3).reshape(l, N_DEV * a, b)
    return g8.transpose(1, 2, 0, 3).reshape(l, a, N_DEV * b)


def _to_shards(full, axis):
    l, a, b = full.shape
    if axis == 1:
        return full.reshape(l, N_DEV, a // N_DEV, b).transpose(1, 0, 2, 3)
    return full.reshape(l, a, N_DEV, b // N_DEV).transpose(2, 0, 1, 3)


def _pack_small(arrs):
    parts, meta, off = [], [], 0
    for a in arrs:
        n = a.size
        npad = -(-n // LANES) * LANES
        parts.append(jnp.pad(a.reshape(-1).astype(F32), (0, npad - n)))
        meta.append((off, n, a.shape))
        off += npad
    rows = -(-(off // LANES) // 8) * 8
    flat = jnp.concatenate(parts)
    flat = jnp.pad(flat, (0, rows * LANES - off))
    return flat.reshape(rows, LANES), meta


def _unpack_small(buf, meta):
    flat = buf.reshape(buf.shape[:-2] + (-1,))
    return [flat[..., off:off + n].reshape(buf.shape[:-2] + tuple(shape)) for off, n, shape in meta]


def _my_shard(full, me, n_local):
    return lax.dynamic_slice_in_dim(full, me * n_local, n_local, axis=full.ndim - 1)


def kernel(*args):
    n_f = len(FWD_NAMES)
    inp = dict(zip(FWD_NAMES, args[:n_f]))
    tgt = args[n_f][0]
    n_w = len(WEIGHTS)
    mom_m = dict(zip(WEIGHTS, args[n_f + 1:n_f + 1 + n_w]))
    mom_v = dict(zip(WEIGHTS, args[n_f + 1 + n_w:n_f + 1 + 2 * n_w]))
    x, ctx, c = inp["x"][0], inp["ctx"][0], inp["c"]
    d = x.shape[1]
    me = 4 * lax.axis_index("x") + 2 * lax.axis_index("y") + lax.axis_index("c")

    buf, meta = _pack_small([c] + [inp[n] for n in SMALL_SHARDED])
    got = _unpack_small(exchange(buf, "gather_small"), meta)
    c_all = got[0].reshape(N_DEV, d)
    small_full = {}
    for n, g8 in zip(SMALL_SHARDED, got[1:]):
        small_full[n] = jnp.moveaxis(g8, 0, -2).reshape(g8.shape[1:-1] + (N_DEV * g8.shape[-1],))
    cond_in = jnp.concatenate([c_all, inp["c_ctx"][None]], axis=0)
    cond = _silu(cond_in)
    cond16 = jnp.pad(cond, ((0, 16 - cond.shape[0]), (0, 0)))

    ada_w = inp["ada_w"]
    n_ada = ada_w.shape[2]
    mod_loc = jnp.stack([_mm(cond16, ada_w[i], "nn", "ada_fwd")[:N_DEV + 1] for i in range(DEPTH)])
    mbuf, mmeta = _pack_small([mod_loc])
    (mod8,) = _unpack_small(exchange(mbuf, "gather_mod"), mmeta)
    mod_full = mod8.transpose(1, 2, 0, 3).reshape(DEPTH, N_DEV + 1, N_DEV * n_ada) + inp["ada_b"][:, None, :]
    mod2 = jnp.stack([mod_full[:, N_DEV], lax.dynamic_index_in_dim(mod_full, me, axis=1, keepdims=False)], axis=1)

    big = list(BIG)
    shards, shapes = {}, {}
    for n in big:
        w16 = inp[n].astype(BF16)
        for i in range(w16.shape[0]):
            key = "%s_%d" % (n, i) if w16.shape[0] > 1 else n
            shards[key], shapes[key] = w16[i], ((1,) + w16.shape[1:], BIG[n])
    first = ["ssd_w_in", "ssd_w_out", "mlp_w1_0", "mlp_w2_0"]
    p = {}
    for n, g8 in zip(first, gather_two_level([shards[n] for n in first], "gather_weights")):
        p[n] = _to_full(g8.reshape((N_DEV,) + shapes[n][0]), shapes[n][1])[0]
    later = {n: s for n, s in shards.items() if n not in first}
    for n, s in later.items():
        rows, cols = s.shape
        p[n] = jnp.zeros((N_DEV * rows, cols) if shapes[n][1] == 1 else (rows, N_DEV * cols), BF16)
    for n in SMALL:
        if n not in ("c_ctx", "ada_b"):
            p[n] = small_full[n] if n in small_full else inp[n]
    for n in ("ssd_conv_w", "ssd_conv_b", "ssd_dt_bias", "ssd_a_log", "ssd_d", "ssd_norm_g", "ret_log_decay", "hgrn_norm_g",
              "gdn_conv_w", "gdn_dt_bias", "gdn_a_log", "gdn_norm_g"):
        p[n] = p[n][0]

    loss_loc, (g_mod2, g_x, g_p) = jax.value_and_grad(
        lambda mod2_, x_, p_: local_loss(x_, ctx, mod2_, tgt, p_, later, shapes), argnums=(0, 1, 2))(mod2, x, p)

    small_g_names = [n for n in SMALL if n not in ("c_ctx", "ada_b")]
    gbuf, gmeta = _pack_small([loss_loc.reshape(1), g_mod2] + [g_p[n] for n in small_g_names])
    g8 = exchange(gbuf, "gather_small_grads")
    gsum = _unpack_small(sum_parts(g8, "sum_small_grads"), gmeta)
    loss = gsum[0][0]
    g_small = dict(zip(small_g_names, gsum[2:]))
    for n in ("ssd_conv_w", "ssd_conv_b", "ssd_dt_bias", "ssd_a_log", "ssd_d", "ssd_norm_g", "ret_log_decay", "hgrn_norm_g",
              "gdn_conv_w", "gdn_dt_bias", "gdn_a_log", "gdn_norm_g"):
        g_small[n] = g_small[n][None]
    dmod_each = _unpack_small(g8, gmeta)[1]
    dmod9 = jnp.concatenate([dmod_each[:, :, 1].transpose(1, 0, 2), gsum[1][:, 0:1]], axis=1)
    g_small["ada_b"] = gsum[1][:, 0] + gsum[1][:, 1]
    dmod16 = jnp.pad(_my_shard(dmod9, me, n_ada), ((0, 0), (0, 16 - dmod9.shape[1]), (0, 0)))
    g_ada_w = jnp.stack([_mm(cond16, dmod16[i], "tn", "ada_dw") for i in range(DEPTH)])
    dcond_part = _mm(dmod16[0], ada_w[0], "nt", "ada_dx")
    for i in range(1, DEPTH):
        dcond_part = dcond_part + _mm(dmod16[i], ada_w[i], "nt", "ada_dx")
    cbuf, cmeta = _pack_small([dcond_part[N_DEV]])
    (dcond8,) = _unpack_small(sum_parts(exchange(cbuf, "gather_dcond"), "sum_dcond"), cmeta)
    g_small["c_ctx"] = jax.vjp(_silu, inp["c_ctx"])[1](dcond8)[0]

    my_core = lax.axis_index("c")
    blocks = []
    for n in big:
        layers = inp[n].shape[0]
        gf = jnp.stack([g_p["%s_%d" % (n, i)] for i in range(layers)]) if layers > 1 else g_p[n][None]
        blocks.append(_to_shards(gf, BIG[n]).reshape(4, 2, -1, inp[n].shape[2]))
    from_sibling = scatter_sibling(blocks, "scatter_grads_sibling")
    sums = [pair_sum(lax.dynamic_index_in_dim(b, my_core, axis=1, keepdims=False), t, "pair_sum_" + n)
            for n, b, t in zip(big, blocks, from_sibling)]
    parts = scatter_chips(sums, "scatter_grads_chips")

    out = {}

    def view(a):
        return a.reshape(-1, a.shape[-1])

    for n, pr in zip(big, parts):
        res = adamw(view(inp[n]), pr, view(mom_m[n]), view(mom_v[n]), "adamw_" + n)
        out[n] = [r.reshape(inp[n].shape) for r in res]
    res = adamw(view(ada_w), view(g_ada_w)[None], view(mom_m["ada_w"]), view(mom_v["ada_w"]), "adamw_ada_w")
    out["ada_w"] = [r.reshape(ada_w.shape) for r in res]
    for n in SMALL_SHARDED:
        g_small[n] = _my_shard(g_small[n], me, inp[n].shape[-1])
    wb, wmeta = _pack_small([inp[n] for n in SMALL])
    gb, _ = _pack_small([g_small[n] for n in SMALL])
    mb, _ = _pack_small([mom_m[n] for n in SMALL])
    vb, _ = _pack_small([mom_v[n] for n in SMALL])
    res = [_unpack_small(r, wmeta) for r in adamw(wb, gb[None], mb, vb, "adamw_small")]
    for j, n in enumerate(SMALL):
        out[n] = [r[j] for r in res]

    outs = [loss, g_x[None]]
    for k in range(4):
        outs += [out[n][k] for n in WEIGHTS]
    return tuple(outs)
```
